```python
import math
import jax
import jax.numpy as jnp
from jax import lax
import numpy as np

D_MODEL = 1024
BATCH = 16
SEQ = 4096
DEPTH = 2
DEC_BATCH = 32
DEC_SEQ = 16
PAST_LEN = 2048

CHUNK = 64
GDN_HEADS = 4
GDN_DK = 128
GDN_DV = 128
CONV_W = 4
SWA_HEADS = 8
SWA_KV_HEADS = 2
SWA_GROUP = SWA_HEADS // SWA_KV_HEADS
SWA_HD = 64
WINDOW = 128
WIN_CHUNKS = WINDOW // CHUNK
N_MEM = 256
XA_HEADS = 4
XA_HD = 128
D_FF = 4 * D_MODEL
N_BRANCH = 3
BRANCH_W = 512
DN_ALPHA = (2.0 * DEPTH) ** 0.25
DN_BETA = (8.0 * DEPTH) ** -0.25
LN_EPS = 1e-5
NORM_EPS = 1e-6

GDN_QK = GDN_HEADS * GDN_DK
GDN_V = GDN_HEADS * GDN_DV
GDN_CONV_CH = 2 * GDN_QK + GDN_V
SWA_Q = SWA_HEADS * SWA_HD
SWA_KV = SWA_KV_HEADS * SWA_HD
XA_Q = XA_HEADS * XA_HD
IN_SPLITS = (GDN_CONV_CH, GDN_HEADS, GDN_HEADS, GDN_V, SWA_Q, SWA_KV, SWA_KV, XA_Q, N_BRANCH * D_MODEL)
D_IN = sum(IN_SPLITS)

kernel_name = 'hybrid_gdn_swa_memxattn_stream_step'


def layer_norm(x, g, b):
    xf = x.astype(jnp.float32)
    mu = jnp.mean(xf, axis=-1, keepdims=True)
    var = jnp.mean(jnp.square(xf - mu), axis=-1, keepdims=True)
    y = (xf - mu) * lax.rsqrt(var + LN_EPS) * g.astype(jnp.float32) + b.astype(jnp.float32)
    return y.astype(x.dtype)


def l2_normalize(x):
    return x * lax.rsqrt(jnp.sum(jnp.square(x), axis=-1, keepdims=True) + NORM_EPS)


def split_in(h):
    offsets = np.cumsum(IN_SPLITS)[:-1].tolist()
    return jnp.split(h, offsets, axis=-1)


def gated_delta_chunked(q, k, v, g, beta, s0):
    B, T, H, DK = q.shape
    DV = v.shape[-1]
    C = min(CHUNK, T)
    N = T // C

    def chunks(t):
        return jnp.transpose(t.reshape(B, N, C, H, t.shape[-1]), (1, 0, 3, 2, 4))

    qc, kc, vc = chunks(q), chunks(k), chunks(v)
    gc = jnp.cumsum(chunks(g[..., None])[..., 0], axis=-1)
    bc = chunks(beta[..., None])
    causal = jnp.tril(jnp.ones((C, C), dtype=bool))
    strict = jnp.tril(jnp.ones((C, C), dtype=bool), -1)
    decay = jnp.exp(jnp.where(causal, gc[..., :, None] - gc[..., None, :], -jnp.inf))
    kb = kc * bc
    a_mat = jnp.where(strict, jnp.einsum('nbhid,nbhjd->nbhij', kb, kc) * decay, 0.0) + jnp.eye(C, dtype=jnp.float32)
    u = lax.linalg.triangular_solve(a_mat, vc * bc, left_side=True, lower=True, unit_diagonal=True)
    w = lax.linalg.triangular_solve(a_mat, kb * jnp.exp(gc)[..., None], left_side=True, lower=True, unit_diagonal=True)
    qk = jnp.einsum('nbhid,nbhjd->nbhij', qc, kc) * decay

    def step(s, inp):
        q_i, k_i, u_i, w_i, qk_i, g_i = inp
        v_new = u_i - jnp.einsum('bhcd,bhde->bhce', w_i, s)
        o_i = (jnp.einsum('bhcd,bhde->bhce', q_i * jnp.exp(g_i)[..., None], s)
               + jnp.einsum('bhij,bhje->bhie', qk_i, v_new))
        g_last = g_i[..., -1:]
        s = (s * jnp.exp(g_last)[..., None]
             + jnp.einsum('bhcd,bhce->bhde', k_i * jnp.exp(g_last - g_i)[..., None], v_new))
        return s, o_i

    s_final, o = lax.scan(step, s0, (qc, kc, u, w, qk, gc))
    o = jnp.transpose(o, (1, 0, 3, 2, 4)).reshape(B, T, H, DV)
    return o, s_final


def gdn_branch(qkv, a, b, z, conv_buf, s0, conv_w, a_log, dt_bias, norm_w):
    B, T, _ = qkv.shape
    xin = jnp.concatenate([conv_buf.astype(qkv.dtype), qkv], axis=1)
    conv = lax.conv_general_dilated(xin, conv_w.astype(xin.dtype)[:, None, :], window_strides=(1,), padding='VALID',
                                    dimension_numbers=('NWC', 'WIO', 'NWC'), feature_group_count=GDN_CONV_CH)
    conv = jax.nn.silu(conv.astype(jnp.float32))
    q, k, v = jnp.split(conv, [GDN_QK, 2 * GDN_QK], axis=-1)
    q = l2_normalize(q.reshape(B, T, GDN_HEADS, GDN_DK)) * (GDN_DK ** -0.5)
    k = l2_normalize(k.reshape(B, T, GDN_HEADS, GDN_DK))
    v = v.reshape(B, T, GDN_HEADS, GDN_DV)
    g = -jnp.exp(a_log.astype(jnp.float32)) * jax.nn.softplus(a.astype(jnp.float32) + dt_bias.astype(jnp.float32))
    beta = jax.nn.sigmoid(b.astype(jnp.float32))
    o, s_final = gated_delta_chunked(q, k, v, g, beta, s0.astype(jnp.float32))
    o = o * lax.rsqrt(jnp.mean(jnp.square(o), axis=-1, keepdims=True) + NORM_EPS) * norm_w.astype(jnp.float32)
    o = o * jax.nn.silu(z.reshape(B, T, GDN_HEADS, GDN_DV).astype(jnp.float32))
    return o.reshape(B, T, GDN_V).astype(qkv.dtype), xin[:, -(CONV_W - 1):], s_final.astype(qkv.dtype)


def sink_attention(q, k, v, valid, sinks):
    s = jnp.einsum('...qkgd,...skd->...kgqs', q, k, preferred_element_type=jnp.float32) * (SWA_HD ** -0.5)
    s = jnp.where(valid, s, -jnp.inf)
    sink = jnp.broadcast_to(sinks.astype(jnp.float32).reshape(SWA_KV_HEADS, SWA_GROUP, 1, 1), s.shape[:-1] + (1,))
    p = jax.nn.softmax(jnp.concatenate([s, sink], axis=-1), axis=-1)[..., :-1]
    return jnp.einsum('...kgqs,...skd->...qkgd', p.astype(v.dtype), v)


def swa_prompt(q, k, v, sinks):
    B, T = q.shape[0], q.shape[1]
    nc = T // CHUNK
    pad = ((0, 0), (WIN_CHUNKS * CHUNK, 0), (0, 0), (0, 0))

    def band(t):
        tp = jnp.pad(t, pad).reshape(B, nc + WIN_CHUNKS, CHUNK, SWA_KV_HEADS, SWA_HD)
        return jnp.concatenate([tp[:, j:j + nc] for j in range(WIN_CHUNKS + 1)], axis=2)

    kb, vb = band(k), band(v)
    qb = q.reshape(B, nc, CHUNK, SWA_KV_HEADS, SWA_GROUP, SWA_HD)
    key_chunk = jnp.arange(nc)[:, None] - WIN_CHUNKS + jnp.arange((WIN_CHUNKS + 1) * CHUNK)[None, :] // CHUNK
    valid = (key_chunk >= 0)[:, None, None, None, :]
    o = sink_attention(qb, kb, vb, valid, sinks)
    return o.reshape(B, T, SWA_Q)


def swa_sample(q, k, v, ck, cv, sinks):
    B, T = q.shape[0], q.shape[1]
    n_keep = ck.shape[1]
    kk = jnp.concatenate([ck.astype(k.dtype), k], axis=1)
    vv = jnp.concatenate([cv.astype(v.dtype), v], axis=1)
    q_pos = PAST_LEN + jnp.arange(T)
    k_pos = jnp.concatenate([PAST_LEN - n_keep + jnp.arange(n_keep), q_pos])
    qc = (q_pos // CHUNK)[:, None]
    kc = (k_pos // CHUNK)[None, :]
    valid = (kc <= qc) & (kc >= qc - WIN_CHUNKS)
    o = sink_attention(q, kk, vv, valid, sinks)
    return o.reshape(B, T, SWA_Q), kk[:, -n_keep:], vv[:, -n_keep:]


def mem_kv(mem, w):
    B, M, _ = mem.shape
    mk, mv = jnp.split(mem @ w, 2, axis=-1)
    return mk.reshape(B, M, XA_HEADS, XA_HD), mv.reshape(B, M, XA_HEADS, XA_HD)


def mem_attention(q, mk, mv):
    B, T = q.shape[0], q.shape[1]
    s = jnp.einsum('bthd,bshd->bhts', q, mk.astype(q.dtype), preferred_element_type=jnp.float32) * (XA_HD ** -0.5)
    p = jax.nn.softmax(s, axis=-1)
    o = jnp.einsum('bhts,bshd->bthd', p.astype(q.dtype), mv.astype(q.dtype))
    return o.reshape(B, T, XA_Q)


def merge_and_channel_mix(x, branches, gate_pre, w_branch, w_o, ln1_g, ln1_b, w_up, w_down, ln2_g, ln2_b):
    gates = jnp.split(jax.nn.sigmoid(gate_pre.astype(jnp.float32)).astype(x.dtype), N_BRANCH, axis=-1)
    merged = gates[0] * (branches[0] @ w_branch[0])
    for i in range(1, N_BRANCH):
        merged = merged + gates[i] * (branches[i] @ w_branch[i])
    x = layer_norm(DN_ALPHA * x + merged @ w_o, ln1_g, ln1_b)
    hidden = jnp.square(jax.nn.relu(x @ w_up))
    return layer_norm(DN_ALPHA * x + hidden @ w_down, ln2_g, ln2_b)


def setup_inputs(seed: int = 0) -> dict:
    key = jax.random.key(seed)
    ks = jax.random.split(key, 24)
    f32 = jnp.float32
    n_keep = min(WINDOW, PAST_LEN)

    def nrm(k, shape, scale=1.0):
        return jax.random.normal(k, shape, f32) * scale

    dt = jnp.exp(jax.random.uniform(ks[10], (DEPTH, GDN_HEADS), f32, minval=math.log(1e-3), maxval=math.log(1e-1)))
    return {
        'x_prompt': nrm(ks[0], (BATCH, SEQ, D_MODEL)),
        'x_sample': nrm(ks[1], (DEC_BATCH, DEC_SEQ, D_MODEL)),
        'state_gdn_s': nrm(ks[2], (DEPTH, DEC_BATCH, GDN_HEADS, GDN_DK, GDN_DV), 0.5),
        'state_gdn_conv': nrm(ks[3], (DEPTH, DEC_BATCH, CONV_W - 1, GDN_CONV_CH)),
        'cache_swa_k': nrm(ks[4], (DEPTH, DEC_BATCH, n_keep, SWA_KV_HEADS, SWA_HD)),
        'cache_swa_v': nrm(ks[5], (DEPTH, DEC_BATCH, n_keep, SWA_KV_HEADS, SWA_HD)),
        'cache_mem_k': nrm(ks[6], (DEPTH, DEC_BATCH, N_MEM, XA_HEADS, XA_HD)),
        'cache_mem_v': nrm(ks[7], (DEPTH, DEC_BATCH, N_MEM, XA_HEADS, XA_HD)),
        'mem_prompt': nrm(ks[8], (BATCH, N_MEM, D_MODEL)),
        'w_in': nrm(ks[9], (DEPTH, D_MODEL, D_IN), D_MODEL ** -0.5),
        'conv_w': nrm(ks[11], (DEPTH, CONV_W, GDN_CONV_CH), CONV_W ** -0.5),
        'a_log': jnp.log(jax.random.uniform(ks[12], (DEPTH, GDN_HEADS), f32, minval=1.0, maxval=16.0)),
        'dt_bias': dt + jnp.log(-jnp.expm1(-dt)),
        'gdn_norm_w': 1.0 + nrm(ks[13], (DEPTH, GDN_DV), 0.02),
        'attn_sinks': nrm(ks[14], (DEPTH, SWA_HEADS), 0.5),
        'w_mem_kv': nrm(ks[15], (DEPTH, D_MODEL, 2 * XA_Q), D_MODEL ** -0.5),
        'w_branch': nrm(ks[16], (DEPTH, N_BRANCH, BRANCH_W, D_MODEL), BRANCH_W ** -0.5),
        'w_o': nrm(ks[17], (DEPTH, D_MODEL, D_MODEL), DN_BETA * D_MODEL ** -0.5),
        'ln1_g': 1.0 + nrm(ks[18], (DEPTH, D_MODEL), 0.02),
        'ln1_b': nrm(ks[19], (DEPTH, D_MODEL), 0.02),
        'w_up': nrm(ks[20], (DEPTH, D_MODEL, D_FF), D_MODEL ** -0.5),
        'w_down': nrm(ks[21], (DEPTH, D_FF, D_MODEL), DN_BETA * D_FF ** -0.5),
        'ln2_g': 1.0 + nrm(ks[22], (DEPTH, D_MODEL), 0.02),
        'ln2_b': nrm(ks[23], (DEPTH, D_MODEL), 0.02),
    }


def reference(x_prompt, x_sample, state_gdn_s, state_gdn_conv, cache_swa_k, cache_swa_v, cache_mem_k, cache_mem_v,
              mem_prompt, w_in, conv_w, a_log, dt_bias, gdn_norm_w, attn_sinks, w_mem_kv, w_branch, w_o,
              ln1_g, ln1_b, w_up, w_down, ln2_g, ln2_b):
    n_keep = cache_swa_k.shape[2]

    x = x_prompt
    B, T, _ = x.shape
    p_s, p_conv, p_k, p_v, p_mk, p_mv = [], [], [], [], [], []
    for l in range(DEPTH):
        qkv, a, b, z, q_s, k_s, v_s, q_x, gate_pre = split_in(x @ w_in[l])
        conv0 = jnp.zeros((B, CONV_W - 1, GDN_CONV_CH), x.dtype)
        s0 = jnp.zeros((B, GDN_HEADS, GDN_DK, GDN_DV), jnp.float32)
        o_g, conv_new, s_new = gdn_branch(qkv, a, b, z, conv0, s0, conv_w[l], a_log[l], dt_bias[l], gdn_norm_w[l])
        k_s = k_s.reshape(B, T, SWA_KV_HEADS, SWA_HD)
        v_s = v_s.reshape(B, T, SWA_KV_HEADS, SWA_HD)
        o_s = swa_prompt(q_s.reshape(B, T, SWA_KV_HEADS, SWA_GROUP, SWA_HD), k_s, v_s, attn_sinks[l])
        mk, mv = mem_kv(mem_prompt, w_mem_kv[l])
        o_x = mem_attention(q_x.reshape(B, T, XA_HEADS, XA_HD), mk, mv)
        x = merge_and_channel_mix(x, (o_g, o_s, o_x), gate_pre, w_branch[l], w_o[l], ln1_g[l], ln1_b[l],
                                  w_up[l], w_down[l], ln2_g[l], ln2_b[l])
        p_s.append(s_new)
        p_conv.append(conv_new)
        p_k.append(k_s[:, T - n_keep:])
        p_v.append(v_s[:, T - n_keep:])
        p_mk.append(mk)
        p_mv.append(mv)
    y_prompt = x

    x = x_sample
    B, T, _ = x.shape
    s_s, s_conv, s_k, s_v = [], [], [], []
    for l in range(DEPTH):
        qkv, a, b, z, q_s, k_s, v_s, q_x, gate_pre = split_in(x @ w_in[l])
        o_g, conv_new, s_new = gdn_branch(qkv, a, b, z, state_gdn_conv[l], state_gdn_s[l], conv_w[l], a_log[l],
                                          dt_bias[l], gdn_norm_w[l])
        o_s, k_buf, v_buf = swa_sample(q_s.reshape(B, T, SWA_KV_HEADS, SWA_GROUP, SWA_HD),
                                       k_s.reshape(B, T, SWA_KV_HEADS, SWA_HD), v_s.reshape(B, T, SWA_KV_HEADS, SWA_HD),
                                       cache_swa_k[l], cache_swa_v[l], attn_sinks[l])
        o_x = mem_attention(q_x.reshape(B, T, XA_HEADS, XA_HD), cache_mem_k[l], cache_mem_v[l])
        x = merge_and_channel_mix(x, (o_g, o_s, o_x), gate_pre, w_branch[l], w_o[l], ln1_g[l], ln1_b[l],
                                  w_up[l], w_down[l], ln2_g[l], ln2_b[l])
        s_s.append(s_new)
        s_conv.append(conv_new)
        s_k.append(k_buf)
        s_v.append(v_buf)
    y_sample = x

    return (y_prompt, y_sample,
            jnp.stack(p_s), jnp.stack(p_conv), jnp.stack(p_k), jnp.stack(p_v), jnp.stack(p_mk), jnp.stack(p_mv),
            jnp.stack(s_s), jnp.stack(s_conv), jnp.stack(s_k), jnp.stack(s_v))
```

```python
import functools

import jax
import jax.numpy as jnp
import numpy as np
from jax import lax
from jax.experimental import pallas as pl
from jax.experimental.pallas import tpu as pltpu

F32 = jnp.float32
BF16 = jnp.bfloat16
HI = lax.Precision.HIGHEST

D_MODEL = 1024
DEPTH = 2
CHUNK = 64
PAST_LEN = 2048
GDN_HEADS = 4
GDN_DK = 128
GDN_DV = 128
CONV_W = 4
SWA_HEADS = 8
SWA_KV_HEADS = 2
SWA_HD = 64
WINDOW = 128
WIN_CHUNKS = WINDOW // CHUNK
N_MEM = 256
XA_HEADS = 4
XA_HD = 128
D_FF = 4 * D_MODEL
N_BRANCH = 3
BRANCH_W = 512
DN_ALPHA = (2.0 * DEPTH) ** 0.25
LN_EPS = 1e-5
NORM_EPS = 1e-6
GDN_QK = GDN_HEADS * GDN_DK
GDN_V = GDN_HEADS * GDN_DV
GDN_CONV_CH = 2 * GDN_QK + GDN_V
SWA_Q = SWA_HEADS * SWA_HD
SWA_KV = SWA_KV_HEADS * SWA_HD
XA_Q = XA_HEADS * XA_HD
IN_SPLITS = (GDN_CONV_CH, GDN_HEADS, GDN_HEADS, GDN_V, SWA_Q, SWA_KV, SWA_KV, XA_Q, N_BRANCH * D_MODEL)

LANES = 128
SUBLANES = 8
TOKEN_TILE = 512
VMEM_LIMIT = 56 * 2**20

SWA_KV_DUP = 2 * SWA_KV
AB_PAD = LANES
PROJ_OUT = (("qkv", GDN_CONV_CH), ("z", GDN_V), ("qs", SWA_Q), ("qx", XA_Q), ("ks", SWA_KV_DUP),
            ("vs", SWA_KV_DUP), ("ab", AB_PAD), ("gate", N_BRANCH * D_MODEL))
PROJ_COLS = sum(w for _, w in PROJ_OUT)
PROJ_CHUNK = 512


def _params(n_grid):
    return pltpu.CompilerParams(dimension_semantics=("arbitrary",) * n_grid, vmem_limit_bytes=VMEM_LIMIT)


def _const_spec(shape):
    zeros = (0,) * len(shape)
    return pl.BlockSpec(shape, lambda *_: zeros, pipeline_mode=pl.Buffered(1))


def _sigmoid(x):
    return 1.0 / (1.0 + jnp.exp(-x))


def _softplus(x):
    return jnp.maximum(x, 0.0) + jnp.log1p(jnp.exp(-jnp.abs(x)))


def _layer_norm(r, g, b):
    mu = jnp.mean(r, axis=-1, keepdims=True)
    d = r - mu
    var = jnp.mean(d * d, axis=-1, keepdims=True)
    return d * lax.rsqrt(var + LN_EPS) * g + b


def _inproj_kernel(x_ref, w_ref, *out_refs):
    xb = x_ref[...].astype(BF16)
    off = 0
    for (name, width), ref in zip(PROJ_OUT, out_refs):
        for c0 in range(0, width, PROJ_CHUNK):
            cw = min(PROJ_CHUNK, width - c0)
            y = jnp.dot(xb, w_ref[:, off + c0:off + c0 + cw], preferred_element_type=F32)
            if name == "gate":
                y = _sigmoid(y)
            ref[:, c0:c0 + cw] = y.astype(ref.dtype)
        off += width


def _in_projection(x2d, w_all, act_dtype):
    m = x2d.shape[0]
    tm = min(TOKEN_TILE, m)
    dtypes = {"ab": F32}
    out_shape = [jax.ShapeDtypeStruct((m, w), dtypes.get(n, act_dtype)) for n, w in PROJ_OUT]
    out_specs = [pl.BlockSpec((tm, w), lambda i: (i, 0)) for _, w in PROJ_OUT]
    outs = pl.pallas_call(
        _inproj_kernel,
        grid=(m // tm,),
        in_specs=[pl.BlockSpec((tm, D_MODEL), lambda i: (i, 0)), _const_spec((D_MODEL, PROJ_COLS))],
        out_specs=out_specs,
        out_shape=out_shape,
        compiler_params=_params(1),
        name="in_projection",
    )(x2d, w_all)
    return dict(zip([n for n, _ in PROJ_OUT], outs))


def _prep_w_in(w):
    offs = np.cumsum((0,) + IN_SPLITS)
    qkv, a, b, z, qs, ks, vs, qx, gate = [w[:, offs[i]:offs[i + 1]] for i in range(len(IN_SPLITS))]

    def dup(t):
        return jnp.concatenate([t[:, :SWA_HD], t[:, :SWA_HD], t[:, SWA_HD:], t[:, SWA_HD:]], axis=1)

    ab = jnp.concatenate([a, b, jnp.zeros((w.shape[0], AB_PAD - 2 * GDN_HEADS), w.dtype)], axis=1)
    return jnp.concatenate([qkv, z, qs, qx, dup(ks), dup(vs), ab, gate], axis=1).astype(BF16)


def _matmul_kernel(x_ref, w_ref, o_ref):
    o_ref[...] = jnp.dot(x_ref[...].astype(BF16), w_ref[...], preferred_element_type=F32)


def _matmul(x2d, w_bf16):
    m, k = x2d.shape
    n = w_bf16.shape[1]
    tm = min(TOKEN_TILE, m)
    return pl.pallas_call(
        _matmul_kernel,
        grid=(m // tm,),
        in_specs=[pl.BlockSpec((tm, k), lambda i: (i, 0)), _const_spec((k, n))],
        out_specs=pl.BlockSpec((tm, n), lambda i: (i, 0)),
        out_shape=jax.ShapeDtypeStruct((m, n), F32),
        compiler_params=_params(1),
        name="mem_kv_projection",
    )(x2d, w_bf16)


def _gdn_kernel(qkv_ref, z_ref, ab_ref, conv0_ref, s0_ref, cw_ref, alog_ref, dtb_ref, nw_ref,
                o_ref, sout_ref, convout_ref, xbuf, cbuf, gbuf, bbuf, s_scr, *, tt, c):
    i = pl.program_id(1)
    hist = SUBLANES

    @pl.when(i == 0)
    def _():
        xbuf[0:hist, :] = conv0_ref[...]
        s_scr[...] = s0_ref[...]

    @pl.when(i > 0)
    def _():
        xbuf[0:hist, :] = xbuf[tt:tt + hist, :]

    xbuf[hist:hist + tt, :] = qkv_ref[...].astype(F32)

    expalog = jnp.exp(alog_ref[...])
    dtb = dtb_ref[...]
    for r0 in range(0, tt, c):
        for cg in range(GDN_CONV_CH // LANES):
            ls = slice(cg * LANES, (cg + 1) * LANES)
            acc = None
            for w in range(CONV_W):
                s = hist - (CONV_W - 1) + w + r0
                term = cw_ref[w:w + 1, ls] * xbuf[s:s + c, ls]
                acc = term if acc is None else acc + term
            y = acc * _sigmoid(acc)
            if cg < 2 * GDN_HEADS:
                y = y * lax.rsqrt(jnp.sum(y * y, axis=-1, keepdims=True) + NORM_EPS)
                if cg < GDN_HEADS:
                    y = y * (GDN_DK ** -0.5)
            cbuf[r0:r0 + c, ls] = y
        ab = ab_ref[r0:r0 + c, :]
        gbuf[r0:r0 + c, :] = -expalog * _softplus(ab + dtb)
        bbuf[r0:r0 + c, :] = _sigmoid(ab)

    row = lax.broadcasted_iota(jnp.int32, (c, c), 0)
    col = lax.broadcasted_iota(jnp.int32, (c, c), 1)
    causal = row >= col
    strict = row > col
    l_incl = causal.astype(F32)
    l_strict = strict.astype(F32)
    eye = (row == col).astype(F32)
    n_square = max(int(np.ceil(np.log2(c))) - 1, 0)
    nw = nw_ref[...]

    def chunk_body(ci, carry):
        r = pl.multiple_of(ci * c, c)
        rows = pl.ds(r, c)
        g_all = gbuf[rows, :]
        b_all = bbuf[rows, :]
        gc_all = jnp.dot(l_incl, g_all, precision=HI, preferred_element_type=F32)
        for h in range(GDN_HEADS):
            q = cbuf[rows, h * GDN_DK:(h + 1) * GDN_DK]
            k = cbuf[rows, GDN_QK + h * GDN_DK:GDN_QK + (h + 1) * GDN_DK]
            v = cbuf[rows, 2 * GDN_QK + h * GDN_DV:2 * GDN_QK + (h + 1) * GDN_DV]
            g_col = g_all[:, h:h + 1]
            beta = b_all[:, GDN_HEADS + h:GDN_HEADS + h + 1]
            gc = gc_all[:, h:h + 1]
            g_last = gc_all[c - 1:c, h:h + 1]
            diff = jnp.dot(l_incl, g_col * l_strict, precision=HI, preferred_element_type=F32)
            decay = jnp.where(causal, jnp.exp(diff), 0.0)
            kb = k * beta
            gram = lax.dot_general(jnp.concatenate([kb, q], axis=0), k, (((1,), (1,)), ((), ())),
                                   precision=HI, preferred_element_type=F32)
            a_neg = -jnp.where(strict, gram[:c] * decay, 0.0)
            qk = gram[c:] * decay
            t_inv = eye + a_neg
            pw = a_neg
            for _ in range(n_square):
                pw = jnp.dot(pw, pw, precision=HI, preferred_element_type=F32)
                t_inv = t_inv + jnp.dot(t_inv, pw, precision=HI, preferred_element_type=F32)
            egc = jnp.exp(gc)
            uw = jnp.dot(t_inv, jnp.concatenate([v * beta, kb * egc], axis=1), precision=HI,
                         preferred_element_type=F32)
            s_old = s_scr[h]
            ws = jnp.dot(jnp.concatenate([uw[:, GDN_DV:], q * egc], axis=0), s_old, precision=HI,
                         preferred_element_type=F32)
            v_new = uw[:, :GDN_DV] - ws[:c]
            o = ws[c:] + jnp.dot(qk, v_new, precision=HI, preferred_element_type=F32)
            k_dec = k * jnp.exp(g_last - gc)
            s_scr[h] = s_old * jnp.exp(g_last) + jnp.dot(k_dec.T, v_new, precision=HI,
                                                          preferred_element_type=F32)
            o = o * lax.rsqrt(jnp.mean(o * o, axis=-1, keepdims=True) + NORM_EPS) * nw
            zz = z_ref[rows, h * GDN_DV:(h + 1) * GDN_DV].astype(F32)
            o_ref[rows, h * GDN_DV:(h + 1) * GDN_DV] = (o * (zz * _sigmoid(zz))).astype(o_ref.dtype)
        return carry

    lax.fori_loop(0, tt // c, chunk_body, 0)

    @pl.when(i == pl.num_programs(1) - 1)
    def _():
        sout_ref[...] = s_scr[...]
        convout_ref[...] = xbuf[hist + tt - (CONV_W - 1):hist + tt, :]


def _gdn(qkv, z, ab, conv0, s0, conv_w, a_log, dt_bias, norm_w, batch, t):
    c = min(CHUNK, t)
    tt = min(TOKEN_TILE, t)
    nt = t // tt
    conv0p = jnp.concatenate(
        [jnp.zeros((batch, SUBLANES - (CONV_W - 1), GDN_CONV_CH), F32), conv0.astype(F32)], axis=1)

    def pad_vec(vv):
        return jnp.zeros((1, LANES), F32).at[0, :vv.shape[0]].set(vv.astype(F32))

    tok = lambda b, i: (b * nt + i, 0)
    o, s_out, conv_out = pl.pallas_call(
        functools.partial(_gdn_kernel, tt=tt, c=c),
        grid=(batch, nt),
        in_specs=[
            pl.BlockSpec((tt, GDN_CONV_CH), tok),
            pl.BlockSpec((tt, GDN_V), tok),
            pl.BlockSpec((tt, AB_PAD), tok),
            pl.BlockSpec((None, SUBLANES, GDN_CONV_CH), lambda b, i: (b, 0, 0)),
            pl.BlockSpec((None, GDN_HEADS, GDN_DK, GDN_DV), lambda b, i: (b, 0, 0, 0)),
            _const_spec((CONV_W, GDN_CONV_CH)),
            _const_spec((1, LANES)),
            _const_spec((1, LANES)),
            _const_spec((1, GDN_DV)),
        ],
        out_specs=[
            pl.BlockSpec((tt, GDN_V), tok),
            pl.BlockSpec((None, GDN_HEADS, GDN_DK, GDN_DV), lambda b, i: (b, 0, 0, 0)),
            pl.BlockSpec((None, CONV_W - 1, GDN_CONV_CH), lambda b, i: (b, 0, 0)),
        ],
        out_shape=[
            jax.ShapeDtypeStruct((batch * t, GDN_V), z.dtype),
            jax.ShapeDtypeStruct((batch, GDN_HEADS, GDN_DK, GDN_DV), F32),
            jax.ShapeDtypeStruct((batch, CONV_W - 1, GDN_CONV_CH), F32),
        ],
        scratch_shapes=[
            pltpu.VMEM((tt + SUBLANES, GDN_CONV_CH), F32),
            pltpu.VMEM((tt, GDN_CONV_CH), F32),
            pltpu.VMEM((tt, LANES), F32),
            pltpu.VMEM((tt, LANES), F32),
            pltpu.VMEM((GDN_HEADS, GDN_DK, GDN_DV), F32),
        ],
        compiler_params=_params(2),
        name="gated_deltanet",
    )(qkv, z, ab, conv0p, s0.astype(F32), conv_w.astype(F32), pad_vec(a_log), pad_vec(dt_bias),
      norm_w.astype(F32).reshape(1, GDN_DV))
    return o, s_out, conv_out


def _swa_kernel(sink_ref, q_ref, kp_ref, kc_ref, vp_ref, vc_ref, o_ref, kbuf, vbuf, *, tq, cq, prev,
                mask_history):
    i = pl.program_id(1)
    kbuf[0:prev, :] = kp_ref[...]
    kbuf[prev:prev + tq, :] = kc_ref[...]
    vbuf[0:prev, :] = vp_ref[...]
    vbuf[prev:prev + tq, :] = vc_ref[...]
    nk = prev + cq
    group = SWA_HEADS // SWA_KV_HEADS
    lane = lax.broadcasted_iota(jnp.int32, (1, LANES), 1)
    low = lane < SWA_HD
    rowi = lax.broadcasted_iota(jnp.int32, (2 * cq, 1), 0)
    top = rowi < cq
    coli = lax.broadcasted_iota(jnp.int32, (1, nk), 1)
    for ci in range(tq // cq):
        r0 = ci * cq
        if mask_history:
            n_missing = jnp.maximum(WIN_CHUNKS - (i * (tq // cq) + ci), 0) * CHUNK
            valid = coli >= n_missing
        for j in range(SWA_KV_HEADS):
            c0 = j * group * SWA_HD
            qst = jnp.concatenate([q_ref[r0:r0 + cq, c0:c0 + LANES],
                                   q_ref[r0:r0 + cq, c0 + LANES:c0 + 2 * LANES]], axis=0).astype(BF16)
            kk = kbuf[r0:r0 + nk, j * LANES:(j + 1) * LANES].astype(BF16)
            vv = vbuf[r0:r0 + nk, j * LANES:(j + 1) * LANES].astype(BF16)
            zero = jnp.zeros_like(kk)
            acc = None
            for half in range(2):
                sel = low if half == 0 else jnp.logical_not(low)
                s = lax.dot_general(qst, jnp.where(sel, kk, zero), (((1,), (1,)), ((), ())),
                                    preferred_element_type=F32) * (SWA_HD ** -0.5)
                if mask_history:
                    s = jnp.where(valid, s, -jnp.inf)
                sink = jnp.where(top, sink_ref[j * group + half], sink_ref[j * group + 2 + half])
                m = jnp.maximum(jnp.max(s, axis=-1, keepdims=True), sink)
                p = jnp.exp(s - m)
                den = jnp.sum(p, axis=-1, keepdims=True) + jnp.exp(sink - m)
                p = (p / den).astype(BF16)
                part = jnp.dot(p, jnp.where(sel, vv, zero), preferred_element_type=F32)
                acc = part if acc is None else acc + part
            o_ref[r0:r0 + cq, c0:c0 + LANES] = acc[:cq].astype(o_ref.dtype)
            o_ref[r0:r0 + cq, c0 + LANES:c0 + 2 * LANES] = acc[cq:].astype(o_ref.dtype)


def _swa_call(sinks, q, k_prev, k_cur, v_prev, v_cur, prev_spec, batch, t, tq, cq, prev, mask_history):
    nt = t // tq
    tok = lambda b, i: (b * nt + i, 0)
    return pl.pallas_call(
        functools.partial(_swa_kernel, tq=tq, cq=cq, prev=prev, mask_history=mask_history),
        grid=(batch, nt),
        in_specs=[
            pl.BlockSpec(memory_space=pltpu.SMEM),
            pl.BlockSpec((tq, SWA_Q), tok),
            prev_spec,
            pl.BlockSpec((tq, SWA_KV_DUP), tok),
            prev_spec,
            pl.BlockSpec((tq, SWA_KV_DUP), tok),
        ],
        out_specs=pl.BlockSpec((tq, SWA_Q), tok),
        out_shape=jax.ShapeDtypeStruct((batch * t, SWA_Q), q.dtype),
        scratch_shapes=[pltpu.VMEM((prev + tq, SWA_KV_DUP), k_cur.dtype),
                        pltpu.VMEM((prev + tq, SWA_KV_DUP), v_cur.dtype)],
        compiler_params=_params(2),
        name="sliding_window_attention",
    )(sinks.astype(F32), q, k_prev, k_cur, v_prev, v_cur)


def _swa_prompt(sinks, q, k, v, batch, t):
    tq = min(TOKEN_TILE, t)
    nt = t // tq
    per = tq // WINDOW
    prev_spec = pl.BlockSpec((WINDOW, SWA_KV_DUP), lambda b, i: (jnp.maximum((b * nt + i) * per - 1, 0), 0))
    return _swa_call(sinks, q, k, k, v, v, prev_spec, batch, t, tq, CHUNK, WINDOW, True)


def _swa_sample(sinks, q, k, v, cache_k, cache_v, batch, t):
    n_keep = cache_k.shape[1]
    q_pos = PAST_LEN + np.arange(t)
    k_pos = np.concatenate([PAST_LEN - n_keep + np.arange(n_keep), q_pos])
    qc = (q_pos // CHUNK)[:, None]
    kc = (k_pos // CHUNK)[None, :]
    assert np.all((kc <= qc) & (kc >= qc - WIN_CHUNKS)), "sample step expects every cached row in window"
    prev_spec = pl.BlockSpec((n_keep, SWA_KV_DUP), lambda b, i: (b, 0))
    return _swa_call(sinks, q, cache_k.reshape(batch * n_keep, SWA_KV_DUP), k,
                     cache_v.reshape(batch * n_keep, SWA_KV_DUP), v, prev_spec, batch, t, t, t, n_keep, False)


def _dup_heads(t):
    return jnp.concatenate([t[..., 0, :], t[..., 0, :], t[..., 1, :], t[..., 1, :]], axis=-1)


def _undup_heads(t):
    return jnp.stack([t[:, :SWA_HD], t[:, 2 * SWA_HD:3 * SWA_HD]], axis=1).astype(F32)


def _memattn_kernel(q_ref, mk_ref, mv_ref, o_ref, *, rows):
    tt = q_ref.shape[0]
    for h in range(XA_HEADS):
        ls = slice(h * XA_HD, (h + 1) * XA_HD)
        k = mk_ref[:, ls].astype(BF16)
        v = mv_ref[:, ls].astype(BF16)
        for r0 in range(0, tt, rows):
            q = q_ref[r0:r0 + rows, ls].astype(BF16)
            s = lax.dot_general(q, k, (((1,), (1,)), ((), ())), preferred_element_type=F32) * (XA_HD ** -0.5)
            m = jnp.max(s, axis=-1, keepdims=True)
            p = jnp.exp(s - m)
            p = (p / jnp.sum(p, axis=-1, keepdims=True)).astype(BF16)
            o_ref[r0:r0 + rows, ls] = jnp.dot(p, v, preferred_element_type=F32).astype(o_ref.dtype)


def _mem_attention(q, mk, mv, batch, t):
    tt = min(TOKEN_TILE, t)
    nt = t // tt
    tok = lambda b, i: (b * nt + i, 0)
    mem = pl.BlockSpec((None, N_MEM, XA_Q), lambda b, i: (b, 0, 0))
    return pl.pallas_call(
        functools.partial(_memattn_kernel, rows=min(256, tt)),
        grid=(batch, nt),
        in_specs=[pl.BlockSpec((tt, XA_Q), tok), mem, mem],
        out_specs=pl.BlockSpec((tt, XA_Q), tok),
        out_shape=jax.ShapeDtypeStruct((batch * t, XA_Q), q.dtype),
        compiler_params=_params(2),
        name="memory_attention",
    )(q, mk, mv)


def _merge_kernel(x_ref, og_ref, os_ref, ox_ref, gate_ref, wb_ref, wo_ref, g_ref, b_ref, out_ref):
    merged = None
    for bi, ref in enumerate((og_ref, os_ref, ox_ref)):
        y = jnp.dot(ref[...].astype(BF16), wb_ref[bi], preferred_element_type=F32)
        y = gate_ref[:, bi * D_MODEL:(bi + 1) * D_MODEL].astype(F32) * y
        merged = y if merged is None else merged + y
    u = jnp.dot(merged.astype(BF16), wo_ref[...], preferred_element_type=F32)
    out_ref[...] = _layer_norm(DN_ALPHA * x_ref[...] + u, g_ref[...], b_ref[...])


def _merge(x2d, o_g, o_s, o_x, gate, w_branch, w_o, ln_g, ln_b):
    m = x2d.shape[0]
    tm = min(TOKEN_TILE, m)
    row = lambda w: pl.BlockSpec((tm, w), lambda i: (i, 0))
    return pl.pallas_call(
        _merge_kernel,
        grid=(m // tm,),
        in_specs=[row(D_MODEL), row(BRANCH_W), row(BRANCH_W), row(BRANCH_W), row(N_BRANCH * D_MODEL),
                  _const_spec((N_BRANCH, BRANCH_W, D_MODEL)), _const_spec((D_MODEL, D_MODEL)),
                  _const_spec((1, D_MODEL)), _const_spec((1, D_MODEL))],
        out_specs=row(D_MODEL),
        out_shape=jax.ShapeDtypeStruct((m, D_MODEL), F32),
        compiler_params=_params(1),
        name="branch_merge",
    )(x2d, o_g, o_s, o_x, gate, w_branch, w_o, ln_g.reshape(1, D_MODEL), ln_b.reshape(1, D_MODEL))


FF_CHUNK = 1024


def _ffn_kernel(x_ref, wu_ref, wd_ref, g_ref, b_ref, out_ref):
    x = x_ref[...]
    xb = x.astype(BF16)
    acc = None
    for c0 in range(0, D_FF, FF_CHUNK):
        hid = jnp.dot(xb, wu_ref[:, c0:c0 + FF_CHUNK], preferred_element_type=F32)
        hid = jnp.square(jnp.maximum(hid, 0.0)).astype(BF16)
        part = jnp.dot(hid, wd_ref[c0:c0 + FF_CHUNK, :], preferred_element_type=F32)
        acc = part if acc is None else acc + part
    out_ref[...] = _layer_norm(DN_ALPHA * x + acc, g_ref[...], b_ref[...])


def _ffn(x2d, w_up, w_down, ln_g, ln_b):
    m = x2d.shape[0]
    tm = min(TOKEN_TILE, m)
    row = pl.BlockSpec((tm, D_MODEL), lambda i: (i, 0))
    return pl.pallas_call(
        _ffn_kernel,
        grid=(m // tm,),
        in_specs=[row, _const_spec((D_MODEL, D_FF)), _const_spec((D_FF, D_MODEL)),
                  _const_spec((1, D_MODEL)), _const_spec((1, D_MODEL))],
        out_specs=row,
        out_shape=jax.ShapeDtypeStruct((m, D_MODEL), F32),
        compiler_params=_params(1),
        name="channel_mixer",
    )(x2d, w_up, w_down, ln_g.reshape(1, D_MODEL), ln_b.reshape(1, D_MODEL))


def _layer(x2d, batch, t, act_dtype, lw, conv0, s0, mk, mv, swa_cache):
    h = _in_projection(x2d, lw["w_in"], act_dtype)
    o_g, s_new, conv_new = _gdn(h["qkv"], h["z"], h["ab"], conv0, s0, lw["conv_w"], lw["a_log"], lw["dt_bias"],
                                lw["gdn_norm_w"], batch, t)
    if swa_cache is None:
        o_s = _swa_prompt(lw["attn_sinks"], h["qs"], h["ks"], h["vs"], batch, t)
    else:
        o_s = _swa_sample(lw["attn_sinks"], h["qs"], h["ks"], h["vs"], swa_cache[0], swa_cache[1], batch, t)
    o_x = _mem_attention(h["qx"], mk, mv, batch, t)
    x1 = _merge(x2d, o_g, o_s, o_x, h["gate"], lw["w_branch"], lw["w_o"], lw["ln1_g"], lw["ln1_b"])
    x2 = _ffn(x1, lw["w_up"], lw["w_down"], lw["ln2_g"], lw["ln2_b"])
    return x2, s_new, conv_new, h["ks"], h["vs"]


def kernel(x_prompt, x_sample, state_gdn_s, state_gdn_conv, cache_swa_k, cache_swa_v, cache_mem_k, cache_mem_v,
           mem_prompt, w_in, conv_w, a_log, dt_bias, gdn_norm_w, attn_sinks, w_mem_kv, w_branch, w_o,
           ln1_g, ln1_b, w_up, w_down, ln2_g, ln2_b):
    depth = w_in.shape[0]
    layers = []
    for l in range(depth):
        layers.append(dict(
            w_in=_prep_w_in(w_in[l]), conv_w=conv_w[l], a_log=a_log[l], dt_bias=dt_bias[l],
            gdn_norm_w=gdn_norm_w[l], attn_sinks=attn_sinks[l], w_mem_kv=w_mem_kv[l].astype(BF16),
            w_branch=w_branch[l].astype(BF16), w_o=w_o[l].astype(BF16), ln1_g=ln1_g[l], ln1_b=ln1_b[l],
            w_up=w_up[l].astype(BF16), w_down=w_down[l].astype(BF16), ln2_g=ln2_g[l], ln2_b=ln2_b[l]))

    bp, tp, _ = x_prompt.shape
    n_keep = cache_swa_k.shape[2]
    x = x_prompt.reshape(bp * tp, D_MODEL)
    mem2d = mem_prompt.reshape(bp * N_MEM, D_MODEL)
    p_s, p_conv, p_k, p_v, p_mk, p_mv = [], [], [], [], [], []
    for lw in layers:
        mkv = _matmul(mem2d, lw["w_mem_kv"]).reshape(bp, N_MEM, 2 * XA_Q)
        mk, mv = mkv[..., :XA_Q], mkv[..., XA_Q:]
        conv0 = jnp.zeros((bp, CONV_W - 1, GDN_CONV_CH), F32)
        s0 = jnp.zeros((bp, GDN_HEADS, GDN_DK, GDN_DV), F32)
        x, s_new, conv_new, ks, vs = _layer(x, bp, tp, BF16, lw, conv0, s0, mk, mv, None)
        p_s.append(s_new)
        p_conv.append(conv_new)
        p_k.append(_undup_heads(ks.reshape(bp, tp, SWA_KV_DUP)[:, tp - n_keep:].reshape(bp * n_keep, SWA_KV_DUP))
                   .reshape(bp, n_keep, SWA_KV_HEADS, SWA_HD))
        p_v.append(_undup_heads(vs.reshape(bp, tp, SWA_KV_DUP)[:, tp - n_keep:].reshape(bp * n_keep, SWA_KV_DUP))
                   .reshape(bp, n_keep, SWA_KV_HEADS, SWA_HD))
        p_mk.append(mk.reshape(bp, N_MEM, XA_HEADS, XA_HD))
        p_mv.append(mv.reshape(bp, N_MEM, XA_HEADS, XA_HD))
    y_prompt = x.reshape(bp, tp, D_MODEL)

    bs, ts, _ = x_sample.shape
    x = x_sample.reshape(bs * ts, D_MODEL)
    s_s, s_conv, s_k, s_v = [], [], [], []
    for l, lw in enumerate(layers):
        cache = (_dup_heads(cache_swa_k[l]), _dup_heads(cache_swa_v[l]))
        x, s_new, conv_new, ks, vs = _layer(
            x, bs, ts, F32, lw, state_gdn_conv[l], state_gdn_s[l],
            cache_mem_k[l].reshape(bs, N_MEM, XA_Q), cache_mem_v[l].reshape(bs, N_MEM, XA_Q), cache)
        s_s.append(s_new)
        s_conv.append(conv_new)
        k_new = _undup_heads(ks).reshape(bs, ts, SWA_KV_HEADS, SWA_HD)
        v_new = _undup_heads(vs).reshape(bs, ts, SWA_KV_HEADS, SWA_HD)
        s_k.append(jnp.concatenate([cache_swa_k[l], k_new], axis=1)[:, -n_keep:])
        s_v.append(jnp.concatenate([cache_swa_v[l], v_new], axis=1)[:, -n_keep:])
    y_sample = x.reshape(bs, ts, D_MODEL)

    return (y_prompt, y_sample,
            jnp.stack(p_s), jnp.stack(p_conv), jnp.stack(p_k), jnp.stack(p_v), jnp.stack(p_mk), jnp.stack(p_mv),
            jnp.stack(s_s), jnp.stack(s_conv), jnp.stack(s_k), jnp.stack(s_v))
```

```python
import functools

import jax
import jax.numpy as jnp
import numpy as np
from jax import lax
from jax.experimental import pallas as pl
from jax.experimental.pallas import tpu as pltpu

F32 = jnp.float32
BF16 = jnp.bfloat16
HI = lax.Precision.HIGHEST

D_MODEL = 1024
DEPTH = 2
CHUNK = 64
PAST_LEN = 2048
GDN_HEADS = 4
GDN_DK = 128
GDN_DV = 128
CONV_W = 4
SWA_HEADS = 8
SWA_KV_HEADS = 2
SWA_HD = 64
WINDOW = 128
WIN_CHUNKS = WINDOW // CHUNK
SWA_SCALE = SWA_HD ** -0.5
assert np.log2(SWA_SCALE) == round(np.log2(SWA_SCALE)), "the scale is folded into q before a bf16 cast"
N_MEM = 256
XA_HEADS = 4
XA_HD = 128
D_FF = 4 * D_MODEL
N_BRANCH = 3
BRANCH_W = 512
DN_ALPHA = (2.0 * DEPTH) ** 0.25
LN_EPS = 1e-5
NORM_EPS = 1e-6
GDN_QK = GDN_HEADS * GDN_DK
GDN_V = GDN_HEADS * GDN_DV
GDN_CONV_CH = 2 * GDN_QK + GDN_V
SWA_Q = SWA_HEADS * SWA_HD
SWA_KV = SWA_KV_HEADS * SWA_HD
XA_Q = XA_HEADS * XA_HD
IN_SPLITS = (GDN_CONV_CH, GDN_HEADS, GDN_HEADS, GDN_V, SWA_Q, SWA_KV, SWA_KV, XA_Q, N_BRANCH * D_MODEL)

LANES = 128
SUBLANES = 8
TOKEN_TILE = 512
VMEM_LIMIT = 56 * 2**20

SWA_KV_DUP = 2 * SWA_KV
AB_PAD = LANES
PROJ_OUT = (("qkv", GDN_CONV_CH), ("z", GDN_V), ("qs", SWA_Q), ("qx", XA_Q), ("ks", SWA_KV_DUP),
            ("vs", SWA_KV_DUP), ("ab", AB_PAD), ("gate", N_BRANCH * D_MODEL))
PROJ_COLS = sum(w for _, w in PROJ_OUT)
PROJ_CHUNK = 512


def _params(n_grid):
    return pltpu.CompilerParams(dimension_semantics=("arbitrary",) * n_grid, vmem_limit_bytes=VMEM_LIMIT)


def _const_spec(shape):
    zeros = (0,) * len(shape)
    return pl.BlockSpec(shape, lambda *_: zeros, pipeline_mode=pl.Buffered(1))


def _sigmoid(x):
    return 1.0 / (1.0 + jnp.exp(-x))


def _softplus(x):
    return jnp.maximum(x, 0.0) + jnp.log1p(jnp.exp(-jnp.abs(x)))


def _layer_norm(r, g, b):
    mu = jnp.mean(r, axis=-1, keepdims=True)
    d = r - mu
    var = jnp.mean(d * d, axis=-1, keepdims=True)
    return d * lax.rsqrt(var + LN_EPS) * g + b


def _inproj_kernel(x_ref, w_ref, *out_refs):
    xb = x_ref[...].astype(BF16)
    off = 0
    for (name, width), ref in zip(PROJ_OUT, out_refs):
        for c0 in range(0, width, PROJ_CHUNK):
            cw = min(PROJ_CHUNK, width - c0)
            y = jnp.dot(xb, w_ref[:, off + c0:off + c0 + cw], preferred_element_type=F32)
            if name == "gate":
                y = _sigmoid(y)
            ref[:, c0:c0 + cw] = y.astype(ref.dtype)
        off += width


def _in_projection(x2d, w_all, act_dtype):
    m = x2d.shape[0]
    tm = min(TOKEN_TILE, m)
    dtypes = {"ab": F32}
    out_shape = [jax.ShapeDtypeStruct((m, w), dtypes.get(n, act_dtype)) for n, w in PROJ_OUT]
    out_specs = [pl.BlockSpec((tm, w), lambda i: (i, 0)) for _, w in PROJ_OUT]
    outs = pl.pallas_call(
        _inproj_kernel,
        grid=(m // tm,),
        in_specs=[pl.BlockSpec((tm, D_MODEL), lambda i: (i, 0)), _const_spec((D_MODEL, PROJ_COLS))],
        out_specs=out_specs,
        out_shape=out_shape,
        compiler_params=_params(1),
        name="in_projection",
    )(x2d, w_all)
    return dict(zip([n for n, _ in PROJ_OUT], outs))


def _prep_w_in(w):
    offs = np.cumsum((0,) + IN_SPLITS)
    qkv, a, b, z, qs, ks, vs, qx, gate = [w[:, offs[i]:offs[i + 1]] for i in range(len(IN_SPLITS))]

    def dup(t):
        return jnp.concatenate([t[:, :SWA_HD], t[:, :SWA_HD], t[:, SWA_HD:], t[:, SWA_HD:]], axis=1)

    ab = jnp.concatenate([a, b, jnp.zeros((w.shape[0], AB_PAD - 2 * GDN_HEADS), w.dtype)], axis=1)
    return jnp.concatenate([qkv, z, qs, qx, dup(ks), dup(vs), ab, gate], axis=1).astype(BF16)


def _matmul_kernel(x_ref, w_ref, o_ref):
    o_ref[...] = jnp.dot(x_ref[...].astype(BF16), w_ref[...], preferred_element_type=F32)


def _matmul(x2d, w_bf16):
    m, k = x2d.shape
    n = w_bf16.shape[1]
    tm = min(TOKEN_TILE, m)
    return pl.pallas_call(
        _matmul_kernel,
        grid=(m // tm,),
        in_specs=[pl.BlockSpec((tm, k), lambda i: (i, 0)), _const_spec((k, n))],
        out_specs=pl.BlockSpec((tm, n), lambda i: (i, 0)),
        out_shape=jax.ShapeDtypeStruct((m, n), F32),
        compiler_params=_params(1),
        name="mem_kv_projection",
    )(x2d, w_bf16)


def _gdn_kernel(qkv_ref, z_ref, ab_ref, conv0_ref, s0_ref, cw_ref, alog_ref, dtb_ref, nw_ref,
                o_ref, sout_ref, convout_ref, xbuf, cbuf, gbuf, bbuf, s_scr, u_s, w_s, qe_s, kdt_s, qk_s, el_s,
                *, tt, c, ua):
    i = pl.program_id(1)
    hist = SUBLANES

    @pl.when(i == 0)
    def _():
        xbuf[0:hist, :] = conv0_ref[...]
        s_scr[...] = s0_ref[...]

    @pl.when(i > 0)
    def _():
        xbuf[0:hist, :] = xbuf[tt:tt + hist, :]

    xbuf[hist:hist + tt, :] = qkv_ref[...].astype(F32)

    expalog = jnp.exp(alog_ref[...])
    dtb = dtb_ref[...]
    for r0 in range(0, tt, c):
        for cg in range(GDN_CONV_CH // LANES):
            ls = slice(cg * LANES, (cg + 1) * LANES)
            acc = None
            for w in range(CONV_W):
                s = hist - (CONV_W - 1) + w + r0
                term = cw_ref[w:w + 1, ls] * xbuf[s:s + c, ls]
                acc = term if acc is None else acc + term
            y = acc * _sigmoid(acc)
            if cg < 2 * GDN_HEADS:
                inv = lax.rsqrt(jnp.sum(y * y, axis=-1, keepdims=True) + NORM_EPS)
                if cg < GDN_HEADS:
                    inv = inv * (GDN_DK ** -0.5)
                y = y * inv
            cbuf[r0:r0 + c, ls] = y
        ab = ab_ref[r0:r0 + c, :]
        gbuf[r0:r0 + c, :] = -expalog * _softplus(ab + dtb)
        bbuf[r0:r0 + c, :] = _sigmoid(ab)

    row = lax.broadcasted_iota(jnp.int32, (c, c), 0)
    col = lax.broadcasted_iota(jnp.int32, (c, c), 1)
    causal = row >= col
    strict = row > col
    l_strict = strict.astype(F32)
    l_incl2 = jnp.concatenate([causal.astype(BF16), causal.astype(BF16)], axis=1)
    n_square = max(int(np.ceil(np.log2(c))) - 1, 0)
    nw = nw_ref[...]
    nt_dims = (((1,), (1,)), ((), ()))

    def split2(x):
        hi = x.astype(BF16)
        lo = (x - hi.astype(F32)).astype(BF16)
        return hi, lo

    def phase_a(gi, carry):
        chunks = []
        for cj in range(ua):
            ci = gi * ua + cj
            rows = pl.ds(pl.multiple_of(ci * c, c), c)
            g_hi, g_lo = split2(gbuf[rows, :])
            gc_all = jnp.dot(l_incl2, jnp.concatenate([g_hi, g_lo], axis=0), preferred_element_type=F32)
            chunks.append((ci, rows, g_hi.astype(F32), g_lo.astype(F32), gc_all))
        chains = []
        for ci, rows, g_hi, g_lo, gc_all in chunks:
            el_s[pl.ds(pl.multiple_of(ci * SUBLANES, SUBLANES), SUBLANES), :] = jnp.broadcast_to(
                jnp.exp(gc_all[c - 1:c, :]), (SUBLANES, LANES))
            b_all = bbuf[rows, :]
            for h in range(GDN_HEADS):
                hq = slice(h * GDN_DK, (h + 1) * GDN_DK)
                hk = slice(GDN_QK + h * GDN_DK, GDN_QK + (h + 1) * GDN_DK)
                hv = slice(2 * GDN_QK + h * GDN_DV, 2 * GDN_QK + (h + 1) * GDN_DV)
                beta = b_all[:, GDN_HEADS + h:GDN_HEADS + h + 1]
                steps = jnp.concatenate([(g_hi[:, h:h + 1] * l_strict).astype(BF16),
                                         (g_lo[:, h:h + 1] * l_strict).astype(BF16)], axis=0)
                diff = jnp.dot(l_incl2, steps, preferred_element_type=F32)
                k = cbuf[rows, hk]
                gram = lax.dot_general(
                    jnp.concatenate([(k * beta).astype(BF16), cbuf[rows, hq].astype(BF16)], axis=0),
                    k.astype(BF16), nt_dims, preferred_element_type=F32)
                chains.append(dict(ci=ci, rows=rows, h=h, hq=hq, hk=hk, hv=hv, beta=beta, diff=diff, gram=gram,
                                   gc=gc_all[:, h:h + 1], g_last=gc_all[c - 1:c, h:h + 1]))
        for ch in chains:
            decay = jnp.where(causal, jnp.exp(ch.pop("diff")), 0.0)
            gram = ch.pop("gram")
            qk_s[ch["ci"] * GDN_HEADS + ch["h"]] = (gram[c:] * decay).astype(BF16)
            ch["toff"] = -jnp.where(strict, gram[:c] * decay, 0.0)
            ch["pw"] = ch["toff"].astype(BF16)
        for _ in range(n_square):
            for ch in chains:
                ch["pw"] = jnp.dot(ch["pw"], ch["pw"], preferred_element_type=F32)
            for ch in chains:
                pwb = ch["pw"].astype(BF16)
                ch["toff"] = ch["toff"] + ch["pw"] + jnp.dot(ch["toff"].astype(BF16), pwb,
                                                             preferred_element_type=F32)
                ch["pw"] = pwb
        for ch in chains:
            rows, beta = ch["rows"], ch["beta"]
            egc = jnp.exp(ch["gc"])
            k = cbuf[rows, ch["hk"]]
            rhs = jnp.concatenate([cbuf[rows, ch["hv"]] * beta, k * (beta * egc)], axis=1)
            uw = rhs + jnp.dot(ch["toff"].astype(BF16), rhs.astype(BF16), preferred_element_type=F32)
            u_s[rows, ch["hq"]] = uw[:, :GDN_DV]
            w_s[rows, ch["hq"]] = uw[:, GDN_DV:].astype(BF16)
            qe_s[rows, ch["hq"]] = (cbuf[rows, ch["hq"]] * egc).astype(BF16)
            kdt_s[ch["ci"] * GDN_HEADS + ch["h"]] = (k * jnp.exp(ch["g_last"] - ch["gc"])).T.astype(BF16)
        return carry

    lax.fori_loop(0, tt // (c * ua), phase_a, 0)

    def phase_b(ci, carry):
        rows = pl.ds(pl.multiple_of(ci * c, c), c)
        e_last = el_s[pl.ds(pl.multiple_of(ci * SUBLANES, SUBLANES), SUBLANES), :][0:1, :]
        heads = range(GDN_HEADS)
        hqs = [slice(h * GDN_DK, (h + 1) * GDN_DK) for h in heads]
        s_old = [s_scr[h] for h in heads]
        ws = [jnp.dot(jnp.concatenate([w_s[rows, hqs[h]], qe_s[rows, hqs[h]]], axis=0), s_old[h].astype(BF16),
                      preferred_element_type=F32) for h in heads]
        v_new = [(u_s[rows, hqs[h]] - ws[h][:c]).astype(BF16) for h in heads]
        s_inc = [jnp.dot(kdt_s[ci * GDN_HEADS + h], v_new[h], preferred_element_type=F32) for h in heads]
        o_intra = [jnp.dot(qk_s[ci * GDN_HEADS + h], v_new[h], preferred_element_type=F32) for h in heads]
        for h in heads:
            s_scr[h] = s_old[h] * e_last[:, h:h + 1] + s_inc[h]
            o = ws[h][c:] + o_intra[h]
            o = o * lax.rsqrt(jnp.mean(o * o, axis=-1, keepdims=True) + NORM_EPS) * nw
            zz = z_ref[rows, hqs[h]].astype(F32)
            o_ref[rows, hqs[h]] = (o * (zz * _sigmoid(zz))).astype(o_ref.dtype)
        return carry

    lax.fori_loop(0, tt // c, phase_b, 0)

    @pl.when(i == pl.num_programs(1) - 1)
    def _():
        sout_ref[...] = s_scr[...]
        convout_ref[...] = xbuf[hist + tt - (CONV_W - 1):hist + tt, :]


GDN_PHASE_A_UNROLL = 4


def _gdn(qkv, z, ab, conv0, s0, conv_w, a_log, dt_bias, norm_w, batch, t):
    c = min(CHUNK, t)
    tt = min(TOKEN_TILE, t)
    nt = t // tt
    n_chunks = tt // c
    ua = min(GDN_PHASE_A_UNROLL, n_chunks)
    conv0p = jnp.concatenate(
        [jnp.zeros((batch, SUBLANES - (CONV_W - 1), GDN_CONV_CH), F32), conv0.astype(F32)], axis=1)

    def pad_vec(vv):
        return jnp.zeros((1, LANES), F32).at[0, :vv.shape[0]].set(vv.astype(F32))

    tok = lambda b, i: (b * nt + i, 0)
    o, s_out, conv_out = pl.pallas_call(
        functools.partial(_gdn_kernel, tt=tt, c=c, ua=ua),
        grid=(batch, nt),
        in_specs=[
            pl.BlockSpec((tt, GDN_CONV_CH), tok),
            pl.BlockSpec((tt, GDN_V), tok),
            pl.BlockSpec((tt, AB_PAD), tok),
            pl.BlockSpec((None, SUBLANES, GDN_CONV_CH), lambda b, i: (b, 0, 0)),
            pl.BlockSpec((None, GDN_HEADS, GDN_DK, GDN_DV), lambda b, i: (b, 0, 0, 0)),
            _const_spec((CONV_W, GDN_CONV_CH)),
            _const_spec((1, LANES)),
            _const_spec((1, LANES)),
            _const_spec((1, GDN_DV)),
        ],
        out_specs=[
            pl.BlockSpec((tt, GDN_V), tok),
            pl.BlockSpec((None, GDN_HEADS, GDN_DK, GDN_DV), lambda b, i: (b, 0, 0, 0)),
            pl.BlockSpec((None, CONV_W - 1, GDN_CONV_CH), lambda b, i: (b, 0, 0)),
        ],
        out_shape=[
            jax.ShapeDtypeStruct((batch * t, GDN_V), z.dtype),
            jax.ShapeDtypeStruct((batch, GDN_HEADS, GDN_DK, GDN_DV), F32),
            jax.ShapeDtypeStruct((batch, CONV_W - 1, GDN_CONV_CH), F32),
        ],
        scratch_shapes=[
            pltpu.VMEM((tt + SUBLANES, GDN_CONV_CH), F32),
            pltpu.VMEM((tt, GDN_CONV_CH), F32),
            pltpu.VMEM((tt, LANES), F32),
            pltpu.VMEM((tt, LANES), F32),
            pltpu.VMEM((GDN_HEADS, GDN_DK, GDN_DV), F32),
            pltpu.VMEM((tt, GDN_V), F32),
            pltpu.VMEM((tt, GDN_QK), BF16),
            pltpu.VMEM((tt, GDN_QK), BF16),
            pltpu.VMEM((n_chunks * GDN_HEADS, GDN_DK, c), BF16),
            pltpu.VMEM((n_chunks * GDN_HEADS, c, c), BF16),
            pltpu.VMEM((n_chunks * SUBLANES, LANES), F32),
        ],
        compiler_params=_params(2),
        name="gated_deltanet",
    )(qkv, z, ab, conv0p, s0.astype(F32), conv_w.astype(F32), pad_vec(a_log), pad_vec(dt_bias),
      norm_w.astype(F32).reshape(1, GDN_DV))
    return o, s_out, conv_out


def _swa_kernel(sink_ref, q_ref, kp_ref, kc_ref, vp_ref, vc_ref, o_ref, kbuf, vbuf, *, tq, cq, prev,
                mask_history):
    i = pl.program_id(1)
    kbuf[0:prev, :] = kp_ref[...]
    kbuf[prev:prev + tq, :] = kc_ref[...]
    vbuf[0:prev, :] = vp_ref[...]
    vbuf[prev:prev + tq, :] = vc_ref[...]
    nk = prev + cq
    group = SWA_HEADS // SWA_KV_HEADS
    lane = lax.broadcasted_iota(jnp.int32, (1, LANES), 1)
    low = lane < SWA_HD
    rowi = lax.broadcasted_iota(jnp.int32, (2 * cq, 1), 0)
    top = rowi < cq
    coli = lax.broadcasted_iota(jnp.int32, (1, nk), 1)
    zero = jnp.zeros((nk, LANES), BF16)

    def scores(ci):
        r0 = ci * cq
        units = []
        for j in range(SWA_KV_HEADS):
            c0 = j * group * SWA_HD
            qst = (jnp.concatenate([q_ref[r0:r0 + cq, c0:c0 + LANES],
                                    q_ref[r0:r0 + cq, c0 + LANES:c0 + 2 * LANES]], axis=0).astype(F32)
                   * SWA_SCALE).astype(BF16)
            kk = kbuf[r0:r0 + nk, j * LANES:(j + 1) * LANES].astype(BF16)
            for half in range(2):
                sel = low if half == 0 else jnp.logical_not(low)
                units.append((j, half, lax.dot_general(qst, jnp.where(sel, kk, zero), (((1,), (1,)), ((), ())),
                                                       preferred_element_type=F32)))
        return ci, units

    def finish(ci, units):
        r0 = ci * cq
        masked = mask_history and ci < WIN_CHUNKS
        if masked:
            n_missing = jnp.maximum(WIN_CHUNKS - (i * (tq // cq) + ci), 0) * CHUNK
            valid = coli >= n_missing
        probs = []
        for j, half, s in units:
            if masked:
                s = jnp.where(valid, s, -jnp.inf)
            sink = jnp.where(top, sink_ref[j * group + half], sink_ref[j * group + 2 + half])
            m = jnp.maximum(jnp.max(s, axis=-1, keepdims=True), sink)
            p = jnp.exp(s - m)
            den = jnp.sum(p, axis=-1, keepdims=True) + jnp.exp(sink - m)
            probs.append((p / den).astype(BF16))
        for j in range(SWA_KV_HEADS):
            c0 = j * group * SWA_HD
            vv = vbuf[r0:r0 + nk, j * LANES:(j + 1) * LANES].astype(BF16)
            acc = (jnp.dot(probs[2 * j], jnp.where(low, vv, zero), preferred_element_type=F32)
                   + jnp.dot(probs[2 * j + 1], jnp.where(low, zero, vv), preferred_element_type=F32))
            o_ref[r0:r0 + cq, c0:c0 + LANES] = acc[:cq].astype(o_ref.dtype)
            o_ref[r0:r0 + cq, c0 + LANES:c0 + 2 * LANES] = acc[cq:].astype(o_ref.dtype)

    pending = scores(0)
    for ci in range(1, tq // cq):
        nxt = scores(ci)
        finish(*pending)
        pending = nxt
    finish(*pending)


def _swa_call(sinks, q, k_prev, k_cur, v_prev, v_cur, prev_spec, batch, t, tq, cq, prev, mask_history):
    nt = t // tq
    tok = lambda b, i: (b * nt + i, 0)
    return pl.pallas_call(
        functools.partial(_swa_kernel, tq=tq, cq=cq, prev=prev, mask_history=mask_history),
        grid=(batch, nt),
        in_specs=[
            pl.BlockSpec(memory_space=pltpu.SMEM),
            pl.BlockSpec((tq, SWA_Q), tok),
            prev_spec,
            pl.BlockSpec((tq, SWA_KV_DUP), tok),
            prev_spec,
            pl.BlockSpec((tq, SWA_KV_DUP), tok),
        ],
        out_specs=pl.BlockSpec((tq, SWA_Q), tok),
        out_shape=jax.ShapeDtypeStruct((batch * t, SWA_Q), q.dtype),
        scratch_shapes=[pltpu.VMEM((prev + tq, SWA_KV_DUP), k_cur.dtype),
                        pltpu.VMEM((prev + tq, SWA_KV_DUP), v_cur.dtype)],
        compiler_params=_params(2),
        name="sliding_window_attention",
    )(sinks.astype(F32), q, k_prev, k_cur, v_prev, v_cur)


def _swa_prompt(sinks, q, k, v, batch, t):
    tq = min(TOKEN_TILE, t)
    nt = t // tq
    per = tq // WINDOW
    prev_spec = pl.BlockSpec((WINDOW, SWA_KV_DUP), lambda b, i: (jnp.maximum((b * nt + i) * per - 1, 0), 0))
    return _swa_call(sinks, q, k, k, v, v, prev_spec, batch, t, tq, CHUNK, WINDOW, True)


def _swa_sample(sinks, q, k, v, cache_k, cache_v, batch, t):
    n_keep = cache_k.shape[1]
    q_pos = PAST_LEN + np.arange(t)
    k_pos = np.concatenate([PAST_LEN - n_keep + np.arange(n_keep), q_pos])
    qc = (q_pos // CHUNK)[:, None]
    kc = (k_pos // CHUNK)[None, :]
    assert np.all((kc <= qc) & (kc >= qc - WIN_CHUNKS)), "sample step expects every cached row in window"
    prev_spec = pl.BlockSpec((n_keep, SWA_KV_DUP), lambda b, i: (b, 0))
    return _swa_call(sinks, q, cache_k.reshape(batch * n_keep, SWA_KV_DUP), k,
                     cache_v.reshape(batch * n_keep, SWA_KV_DUP), v, prev_spec, batch, t, t, t, n_keep, False)


def _dup_heads(t):
    return jnp.concatenate([t[..., 0, :], t[..., 0, :], t[..., 1, :], t[..., 1, :]], axis=-1)


def _undup_heads(t):
    return jnp.stack([t[:, :SWA_HD], t[:, 2 * SWA_HD:3 * SWA_HD]], axis=1).astype(F32)


def _memattn_kernel(q_ref, mk_ref, mv_ref, o_ref, *, rows):
    tt = q_ref.shape[0]

    def scores(h, r0):
        ls = slice(h * XA_HD, (h + 1) * XA_HD)
        s = lax.dot_general(q_ref[r0:r0 + rows, ls].astype(BF16), mk_ref[:, ls].astype(BF16),
                            (((1,), (1,)), ((), ())), preferred_element_type=F32)
        return h, r0, s

    def finish(h, r0, s):
        ls = slice(h * XA_HD, (h + 1) * XA_HD)
        s = s * (XA_HD ** -0.5)
        m = jnp.max(s, axis=-1, keepdims=True)
        p = jnp.exp(s - m)
        p = (p / jnp.sum(p, axis=-1, keepdims=True)).astype(BF16)
        o_ref[r0:r0 + rows, ls] = jnp.dot(p, mv_ref[:, ls].astype(BF16),
                                          preferred_element_type=F32).astype(o_ref.dtype)

    blocks = [(h, r0) for h in range(XA_HEADS) for r0 in range(0, tt, rows)]
    pending = scores(*blocks[0])
    for blk in blocks[1:]:
        nxt = scores(*blk)
        finish(*pending)
        pending = nxt
    finish(*pending)


def _mem_attention(q, mk, mv, batch, t):
    tt = min(TOKEN_TILE, t)
    nt = t // tt
    tok = lambda b, i: (b * nt + i, 0)
    mem = pl.BlockSpec((None, N_MEM, XA_Q), lambda b, i: (b, 0, 0))
    return pl.pallas_call(
        functools.partial(_memattn_kernel, rows=min(256, tt)),
        grid=(batch, nt),
        in_specs=[pl.BlockSpec((tt, XA_Q), tok), mem, mem],
        out_specs=pl.BlockSpec((tt, XA_Q), tok),
        out_shape=jax.ShapeDtypeStruct((batch * t, XA_Q), q.dtype),
        compiler_params=_params(2),
        name="memory_attention",
    )(q, mk, mv)


def _merge_kernel(x_ref, og_ref, os_ref, ox_ref, gate_ref, wb_ref, wo_ref, g_ref, b_ref, out_ref):
    merged = None
    for bi, ref in enumerate((og_ref, os_ref, ox_ref)):
        y = jnp.dot(ref[...].astype(BF16), wb_ref[bi], preferred_element_type=F32)
        y = gate_ref[:, bi * D_MODEL:(bi + 1) * D_MODEL].astype(F32) * y
        merged = y if merged is None else merged + y
    u = jnp.dot(merged.astype(BF16), wo_ref[...], preferred_element_type=F32)
    out_ref[...] = _layer_norm(DN_ALPHA * x_ref[...] + u, g_ref[...], b_ref[...])


def _merge(x2d, o_g, o_s, o_x, gate, w_branch, w_o, ln_g, ln_b):
    m = x2d.shape[0]
    tm = min(TOKEN_TILE, m)
    row = lambda w: pl.BlockSpec((tm, w), lambda i: (i, 0))
    return pl.pallas_call(
        _merge_kernel,
        grid=(m // tm,),
        in_specs=[row(D_MODEL), row(BRANCH_W), row(BRANCH_W), row(BRANCH_W), row(N_BRANCH * D_MODEL),
                  _const_spec((N_BRANCH, BRANCH_W, D_MODEL)), _const_spec((D_MODEL, D_MODEL)),
                  _const_spec((1, D_MODEL)), _const_spec((1, D_MODEL))],
        out_specs=row(D_MODEL),
        out_shape=jax.ShapeDtypeStruct((m, D_MODEL), F32),
        compiler_params=_params(1),
        name="branch_merge",
    )(x2d, o_g, o_s, o_x, gate, w_branch, w_o, ln_g.reshape(1, D_MODEL), ln_b.reshape(1, D_MODEL))


FF_CHUNK = 1024


def _ffn_kernel(x_ref, wu_ref, wd_ref, g_ref, b_ref, out_ref):
    x = x_ref[...]
    xb = x.astype(BF16)
    acc = None
    for c0 in range(0, D_FF, FF_CHUNK):
        hid = jnp.dot(xb, wu_ref[:, c0:c0 + FF_CHUNK], preferred_element_type=F32)
        hid = jnp.square(jnp.maximum(hid, 0.0)).astype(BF16)
        part = jnp.dot(hid, wd_ref[c0:c0 + FF_CHUNK, :], preferred_element_type=F32)
        acc = part if acc is None else acc + part
    out_ref[...] = _layer_norm(DN_ALPHA * x + acc, g_ref[...], b_ref[...])


def _ffn(x2d, w_up, w_down, ln_g, ln_b):
    m = x2d.shape[0]
    tm = min(TOKEN_TILE, m)
    row = pl.BlockSpec((tm, D_MODEL), lambda i: (i, 0))
    return pl.pallas_call(
        _ffn_kernel,
        grid=(m // tm,),
        in_specs=[row, _const_spec((D_MODEL, D_FF)), _const_spec((D_FF, D_MODEL)),
                  _const_spec((1, D_MODEL)), _const_spec((1, D_MODEL))],
        out_specs=row,
        out_shape=jax.ShapeDtypeStruct((m, D_MODEL), F32),
        compiler_params=_params(1),
        name="channel_mixer",
    )(x2d, w_up, w_down, ln_g.reshape(1, D_MODEL), ln_b.reshape(1, D_MODEL))


def _layer(x2d, batch, t, act_dtype, lw, conv0, s0, mk, mv, swa_cache):
    h = _in_projection(x2d, lw["w_in"], act_dtype)
    o_g, s_new, conv_new = _gdn(h["qkv"], h["z"], h["ab"], conv0, s0, lw["conv_w"], lw["a_log"], lw["dt_bias"],
                                lw["gdn_norm_w"], batch, t)
    if swa_cache is None:
        o_s = _swa_prompt(lw["attn_sinks"], h["qs"], h["ks"], h["vs"], batch, t)
    else:
        o_s = _swa_sample(lw["attn_sinks"], h["qs"], h["ks"], h["vs"], swa_cache[0], swa_cache[1], batch, t)
    o_x = _mem_attention(h["qx"], mk, mv, batch, t)
    x1 = _merge(x2d, o_g, o_s, o_x, h["gate"], lw["w_branch"], lw["w_o"], lw["ln1_g"], lw["ln1_b"])
    x2 = _ffn(x1, lw["w_up"], lw["w_down"], lw["ln2_g"], lw["ln2_b"])
    return x2, s_new, conv_new, h["ks"], h["vs"]


def kernel(x_prompt, x_sample, state_gdn_s, state_gdn_conv, cache_swa_k, cache_swa_v, cache_mem_k, cache_mem_v,
           mem_prompt, w_in, conv_w, a_log, dt_bias, gdn_norm_w, attn_sinks, w_mem_kv, w_branch, w_o,
           ln1_g, ln1_b, w_up, w_down, ln2_g, ln2_b):
    depth = w_in.shape[0]
    layers = []
    for l in range(depth):
        layers.append(dict(
            w_in=_prep_w_in(w_in[l]), conv_w=conv_w[l], a_log=a_log[l], dt_bias=dt_bias[l],
            gdn_norm_w=gdn_norm_w[l], attn_sinks=attn_sinks[l], w_mem_kv=w_mem_kv[l].astype(BF16),
            w_branch=w_branch[l].astype(BF16), w_o=w_o[l].astype(BF16), ln1_g=ln1_g[l], ln1_b=ln1_b[l],
            w_up=w_up[l].astype(BF16), w_down=w_down[l].astype(BF16), ln2_g=ln2_g[l], ln2_b=ln2_b[l]))

    bp, tp, _ = x_prompt.shape
    n_keep = cache_swa_k.shape[2]
    x = x_prompt.reshape(bp * tp, D_MODEL)
    mem2d = mem_prompt.reshape(bp * N_MEM, D_MODEL)
    p_s, p_conv, p_k, p_v, p_mk, p_mv = [], [], [], [], [], []
    for lw in layers:
        mkv = _matmul(mem2d, lw["w_mem_kv"]).reshape(bp, N_MEM, 2 * XA_Q)
        mk, mv = mkv[..., :XA_Q], mkv[..., XA_Q:]
        conv0 = jnp.zeros((bp, CONV_W - 1, GDN_CONV_CH), F32)
        s0 = jnp.zeros((bp, GDN_HEADS, GDN_DK, GDN_DV), F32)
        x, s_new, conv_new, ks, vs = _layer(x, bp, tp, BF16, lw, conv0, s0, mk, mv, None)
        p_s.append(s_new)
        p_conv.append(conv_new)
        p_k.append(_undup_heads(ks.reshape(bp, tp, SWA_KV_DUP)[:, tp - n_keep:].reshape(bp * n_keep, SWA_KV_DUP))
                   .reshape(bp, n_keep, SWA_KV_HEADS, SWA_HD))
        p_v.append(_undup_heads(vs.reshape(bp, tp, SWA_KV_DUP)[:, tp - n_keep:].reshape(bp * n_keep, SWA_KV_DUP))
                   .reshape(bp, n_keep, SWA_KV_HEADS, SWA_HD))
        p_mk.append(mk.reshape(bp, N_MEM, XA_HEADS, XA_HD))
        p_mv.append(mv.reshape(bp, N_MEM, XA_HEADS, XA_HD))
    y_prompt = x.reshape(bp, tp, D_MODEL)

    bs, ts, _ = x_sample.shape
    x = x_sample.reshape(bs * ts, D_MODEL)
    s_s, s_conv, s_k, s_v = [], [], [], []
    for l, lw in enumerate(layers):
        cache = (_dup_heads(cache_swa_k[l]), _dup_heads(cache_swa_v[l]))
        x, s_new, conv_new, ks, vs = _layer(
            x, bs, ts, F32, lw, state_gdn_conv[l], state_gdn_s[l],
            cache_mem_k[l].reshape(bs, N_MEM, XA_Q), cache_mem_v[l].reshape(bs, N_MEM, XA_Q), cache)
        s_s.append(s_new)
        s_conv.append(conv_new)
        k_new = _undup_heads(ks).reshape(bs, ts, SWA_KV_HEADS, SWA_HD)
        v_new = _undup_heads(vs).reshape(bs, ts, SWA_KV_HEADS, SWA_HD)
        s_k.append(jnp.concatenate([cache_swa_k[l], k_new], axis=1)[:, -n_keep:])
        s_v.append(jnp.concatenate([cache_swa_v[l], v_new], axis=1)[:, -n_keep:])
    y_sample = x.reshape(bs, ts, D_MODEL)

    return (y_prompt, y_sample,
            jnp.stack(p_s), jnp.stack(p_conv), jnp.stack(p_k), jnp.stack(p_v), jnp.stack(p_mk), jnp.stack(p_mv),
            jnp.stack(s_s), jnp.stack(s_conv), jnp.stack(s_k), jnp.stack(s_v))
```

```python
import functools

import jax
import jax.numpy as jnp
import numpy as np
from jax import lax
from jax.experimental import pallas as pl
from jax.experimental.pallas import tpu as pltpu

F32 = jnp.float32
BF16 = jnp.bfloat16
HI = lax.Precision.HIGHEST

D_MODEL = 1024
DEPTH = 2
CHUNK = 64
PAST_LEN = 2048
GDN_HEADS = 4
GDN_DK = 128
GDN_DV = 128
CONV_W = 4
SWA_HEADS = 8
SWA_KV_HEADS = 2
SWA_HD = 64
WINDOW = 128
WIN_CHUNKS = WINDOW // CHUNK
SWA_SCALE = SWA_HD ** -0.5
assert np.log2(SWA_SCALE) == round(np.log2(SWA_SCALE)), "the scale is folded into q before a bf16 cast"
N_MEM = 256
XA_HEADS = 4
XA_HD = 128
D_FF = 4 * D_MODEL
N_BRANCH = 3
BRANCH_W = 512
DN_ALPHA = (2.0 * DEPTH) ** 0.25
LN_EPS = 1e-5
NORM_EPS = 1e-6
GDN_QK = GDN_HEADS * GDN_DK
GDN_V = GDN_HEADS * GDN_DV
GDN_CONV_CH = 2 * GDN_QK + GDN_V
SWA_Q = SWA_HEADS * SWA_HD
SWA_KV = SWA_KV_HEADS * SWA_HD
XA_Q = XA_HEADS * XA_HD
IN_SPLITS = (GDN_CONV_CH, GDN_HEADS, GDN_HEADS, GDN_V, SWA_Q, SWA_KV, SWA_KV, XA_Q, N_BRANCH * D_MODEL)

LANES = 128
SUBLANES = 8
TOKEN_TILE = 512
VMEM_LIMIT = 56 * 2**20

SWA_KV_DUP = 2 * SWA_KV
AB_PAD = LANES
PROJ_OUT = (("qkv", GDN_CONV_CH), ("z", GDN_V), ("qs", SWA_Q), ("qx", XA_Q), ("ks", SWA_KV_DUP),
            ("vs", SWA_KV_DUP), ("ab", AB_PAD), ("gate", N_BRANCH * D_MODEL))
PROJ_COLS = sum(w for _, w in PROJ_OUT)
PROJ_CHUNK = 512


def _params(n_grid):
    return pltpu.CompilerParams(dimension_semantics=("arbitrary",) * n_grid, vmem_limit_bytes=VMEM_LIMIT)


def _const_spec(shape):
    zeros = (0,) * len(shape)
    return pl.BlockSpec(shape, lambda *_: zeros, pipeline_mode=pl.Buffered(1))


def _layer_spec(shape, layer):
    zeros = (0,) * len(shape)
    return pl.BlockSpec((None,) + tuple(shape), lambda *_: (layer,) + zeros, pipeline_mode=pl.Buffered(1))


def _sigmoid(x):
    return 1.0 / (1.0 + jnp.exp(-x))


def _softplus(x):
    return jnp.maximum(x, 0.0) + jnp.log1p(jnp.exp(-jnp.abs(x)))


def _layer_norm(r, g, b):
    mu = jnp.mean(r, axis=-1, keepdims=True)
    d = r - mu
    var = jnp.mean(d * d, axis=-1, keepdims=True)
    return d * lax.rsqrt(var + LN_EPS) * g + b


def _inproj_kernel(x_ref, w_ref, *out_refs):
    xb = x_ref[...].astype(BF16)
    off = 0
    for (name, width), ref in zip(PROJ_OUT, out_refs):
        for c0 in range(0, width, PROJ_CHUNK):
            cw = min(PROJ_CHUNK, width - c0)
            y = jnp.dot(xb, w_ref[:, off + c0:off + c0 + cw], preferred_element_type=F32)
            if name == "gate":
                y = _sigmoid(y)
            ref[:, c0:c0 + cw] = y.astype(ref.dtype)
        off += width


def _in_projection(x2d, w_all, layer, act_dtype):
    m = x2d.shape[0]
    tm = min(TOKEN_TILE, m)
    dtypes = {"ab": F32}
    out_shape = [jax.ShapeDtypeStruct((m, w), dtypes.get(n, act_dtype)) for n, w in PROJ_OUT]
    out_specs = [pl.BlockSpec((tm, w), lambda i: (i, 0)) for _, w in PROJ_OUT]
    outs = pl.pallas_call(
        _inproj_kernel,
        grid=(m // tm,),
        in_specs=[pl.BlockSpec((tm, D_MODEL), lambda i: (i, 0)), _layer_spec((D_MODEL, PROJ_COLS), layer)],
        out_specs=out_specs,
        out_shape=out_shape,
        compiler_params=_params(1),
        name="in_projection",
    )(x2d, w_all)
    return dict(zip([n for n, _ in PROJ_OUT], outs))


def _prep_w_in(w):
    offs = np.cumsum((0,) + IN_SPLITS)
    qkv, a, b, z, qs, ks, vs, qx, gate = [w[..., offs[i]:offs[i + 1]] for i in range(len(IN_SPLITS))]

    def dup(t):
        return [t[..., :SWA_HD], t[..., :SWA_HD], t[..., SWA_HD:], t[..., SWA_HD:]]

    pad = jnp.zeros(w.shape[:-1] + (AB_PAD - 2 * GDN_HEADS,), w.dtype)
    return jnp.concatenate([qkv, z, qs, qx, *dup(ks), *dup(vs), a, b, pad, gate], axis=-1).astype(BF16)


def _matmul_kernel(x_ref, w_ref, o_ref):
    o_ref[...] = jnp.dot(x_ref[...].astype(BF16), w_ref[...], preferred_element_type=F32)


def _matmul(x2d, w_bf16, layer):
    m, k = x2d.shape
    n = w_bf16.shape[-1]
    tm = min(TOKEN_TILE, m)
    return pl.pallas_call(
        _matmul_kernel,
        grid=(m // tm,),
        in_specs=[pl.BlockSpec((tm, k), lambda i: (i, 0)), _layer_spec((k, n), layer)],
        out_specs=pl.BlockSpec((tm, n), lambda i: (i, 0)),
        out_shape=jax.ShapeDtypeStruct((m, n), F32),
        compiler_params=_params(1),
        name="mem_kv_projection",
    )(x2d, w_bf16)


def _gdn_kernel(qkv_ref, prev_ref, z_ref, ab_ref, conv0_ref, s0_ref, cw_ref, alog_ref, dtb_ref, nw_ref,
                o_ref, sout_ref, convout_ref, xbuf, cbuf, gbuf, bbuf, s_scr, u_s, w_s, qe_s, kdt_s, qk_s, el_s,
                *, tt, c, ua, zero_history):
    i = pl.program_id(1)
    hist = SUBLANES

    @pl.when(i == 0)
    def _():
        s_scr[...] = s0_ref[...]

    if zero_history:
        sel_r = lax.broadcasted_iota(jnp.int32, ((CONV_W - 1) * c, c + PREV_ROWS), 0)
        sel_j = lax.broadcasted_iota(jnp.int32, ((CONV_W - 1) * c, c + PREV_ROWS), 1)
        shift_sel = (sel_j == sel_r % c + PREV_ROWS - (sel_r // c + 1)).astype(BF16)
        prev_rows = jnp.where(i > 0, prev_ref[...], jnp.zeros_like(prev_ref[...]))
    else:
        @pl.when(i == 0)
        def _():
            xbuf[0:hist, :] = conv0_ref[...]

        @pl.when(i > 0)
        def _():
            xbuf[0:hist, :] = xbuf[tt:tt + hist, :]

        xbuf[hist:hist + tt, :] = qkv_ref[...].astype(F32)

    expalog = jnp.exp(alog_ref[...])
    dtb = dtb_ref[...]

    def prepass(ci):
        r0 = ci * c
        if zero_history:
            blk = (jnp.concatenate([prev_rows, qkv_ref[0:c, :]], axis=0) if ci == 0
                   else qkv_ref[r0 - PREV_ROWS:r0 + c, :])
            shifted = [jnp.dot(shift_sel, blk[:, g0:g0 + SHIFT_COLS], preferred_element_type=F32)
                       for g0 in range(0, GDN_CONV_CH, SHIFT_COLS)]
        for cg in range(GDN_CONV_CH // LANES):
            ls = slice(cg * LANES, (cg + 1) * LANES)
            if zero_history:
                g, off = divmod(cg * LANES, SHIFT_COLS)
                acc = cw_ref[CONV_W - 1:CONV_W, ls] * qkv_ref[r0:r0 + c, ls].astype(F32)
                for s in range(1, CONV_W):
                    acc = acc + (cw_ref[CONV_W - 1 - s:CONV_W - s, ls]
                                 * shifted[g][(s - 1) * c:s * c, off:off + LANES])
            else:
                acc = None
                for w in range(CONV_W):
                    s = hist - (CONV_W - 1) + w + r0
                    term = cw_ref[w:w + 1, ls] * xbuf[s:s + c, ls]
                    acc = term if acc is None else acc + term
            y = acc * _sigmoid(acc)
            if cg < 2 * GDN_HEADS:
                inv = lax.rsqrt(jnp.sum(y * y, axis=-1, keepdims=True) + NORM_EPS)
                if cg < GDN_HEADS:
                    inv = inv * (GDN_DK ** -0.5)
                y = y * inv
            cbuf[r0:r0 + c, ls] = y
        ab = ab_ref[r0:r0 + c, :]
        gbuf[r0:r0 + c, :] = -expalog * _softplus(ab + dtb)
        bbuf[r0:r0 + c, :] = _sigmoid(ab)

    row = lax.broadcasted_iota(jnp.int32, (c, c), 0)
    col = lax.broadcasted_iota(jnp.int32, (c, c), 1)
    causal = row >= col
    strict = row > col
    l_strict = strict.astype(F32)
    l_incl2 = jnp.concatenate([causal.astype(BF16), causal.astype(BF16)], axis=1)
    n_square = max(int(np.ceil(np.log2(c))) - 1, 0)
    nw = nw_ref[...]
    nt_dims = (((1,), (1,)), ((), ()))

    def split2(x):
        hi = x.astype(BF16)
        lo = (x - hi.astype(F32)).astype(BF16)
        return hi, lo

    def phase_a(chunk_ids):
        chunks = []
        for ci in chunk_ids:
            rows = slice(ci * c, (ci + 1) * c)
            g_hi, g_lo = split2(gbuf[rows, :])
            gc_all = jnp.dot(l_incl2, jnp.concatenate([g_hi, g_lo], axis=0), preferred_element_type=F32)
            chunks.append((ci, rows, g_hi.astype(F32), g_lo.astype(F32), gc_all))
        yield
        chains = []
        for ci, rows, g_hi, g_lo, gc_all in chunks:
            el_s[ci * SUBLANES:(ci + 1) * SUBLANES, :] = jnp.broadcast_to(jnp.exp(gc_all[c - 1:c, :]),
                                                                         (SUBLANES, LANES))
            b_all = bbuf[rows, :]
            for h in range(GDN_HEADS):
                hq = slice(h * GDN_DK, (h + 1) * GDN_DK)
                hk = slice(GDN_QK + h * GDN_DK, GDN_QK + (h + 1) * GDN_DK)
                hv = slice(2 * GDN_QK + h * GDN_DV, 2 * GDN_QK + (h + 1) * GDN_DV)
                beta = b_all[:, GDN_HEADS + h:GDN_HEADS + h + 1]
                steps = jnp.concatenate([(g_hi[:, h:h + 1] * l_strict).astype(BF16),
                                         (g_lo[:, h:h + 1] * l_strict).astype(BF16)], axis=0)
                diff = jnp.dot(l_incl2, steps, preferred_element_type=F32)
                k = cbuf[rows, hk]
                gram = lax.dot_general(
                    jnp.concatenate([(k * beta).astype(BF16), cbuf[rows, hq].astype(BF16)], axis=0),
                    k.astype(BF16), nt_dims, preferred_element_type=F32)
                chains.append(dict(ci=ci, rows=rows, h=h, hq=hq, hk=hk, hv=hv, beta=beta, diff=diff, gram=gram,
                                   gc=gc_all[:, h:h + 1], g_last=gc_all[c - 1:c, h:h + 1]))
        yield
        for ch in chains:
            decay = jnp.where(causal, jnp.exp(ch.pop("diff")), 0.0)
            gram = ch.pop("gram")
            qk_s[ch["ci"] * GDN_HEADS + ch["h"]] = (gram[c:] * decay).astype(BF16)
            ch["toff"] = -jnp.where(strict, gram[:c] * decay, 0.0)
            ch["pw"] = ch["toff"].astype(BF16)
        for _ in range(n_square):
            for ch in chains:
                ch["pw"] = jnp.dot(ch["pw"], ch["pw"], preferred_element_type=F32)
            yield
            for ch in chains:
                pwb = ch["pw"].astype(BF16)
                ch["toff"] = ch["toff"] + ch["pw"] + jnp.dot(ch["toff"].astype(BF16), pwb,
                                                             preferred_element_type=F32)
                ch["pw"] = pwb
            yield
        for ch in chains:
            rows, beta = ch["rows"], ch["beta"]
            egc = jnp.exp(ch["gc"])
            k = cbuf[rows, ch["hk"]]
            rhs = jnp.concatenate([cbuf[rows, ch["hv"]] * beta, k * (beta * egc)], axis=1)
            uw = rhs + jnp.dot(ch["toff"].astype(BF16), rhs.astype(BF16), preferred_element_type=F32)
            u_s[rows, ch["hq"]] = uw[:, :GDN_DV]
            w_s[rows, ch["hq"]] = uw[:, GDN_DV:].astype(BF16)
            qe_s[rows, ch["hq"]] = (cbuf[rows, ch["hq"]] * egc).astype(BF16)
            kdt_s[ch["ci"] * GDN_HEADS + ch["h"]] = (k * jnp.exp(ch["g_last"] - ch["gc"])).T.astype(BF16)
        yield

    heads = range(GDN_HEADS)
    hqs = [slice(h * GDN_DK, (h + 1) * GDN_DK) for h in heads]
    state = [s_scr[h] for h in heads]

    def phase_b(chunk_ids):
        for ci in chunk_ids:
            rows = slice(ci * c, (ci + 1) * c)
            e_last = el_s[ci * SUBLANES:ci * SUBLANES + 1, :]
            ws = [jnp.dot(jnp.concatenate([w_s[rows, hqs[h]], qe_s[rows, hqs[h]]], axis=0),
                          state[h].astype(BF16), preferred_element_type=F32) for h in heads]
            yield
            v_new = [(u_s[rows, hqs[h]] - ws[h][:c]).astype(BF16) for h in heads]
            s_inc = [jnp.dot(kdt_s[ci * GDN_HEADS + h], v_new[h], preferred_element_type=F32) for h in heads]
            o_intra = [jnp.dot(qk_s[ci * GDN_HEADS + h], v_new[h], preferred_element_type=F32) for h in heads]
            yield
            for h in heads:
                state[h] = state[h] * e_last[:, h:h + 1] + s_inc[h]
                o = ws[h][c:] + o_intra[h]
                o = o * lax.rsqrt(jnp.mean(o * o, axis=-1, keepdims=True) + NORM_EPS) * nw
                zz = z_ref[rows, hqs[h]].astype(F32)
                o_ref[rows, hqs[h]] = (o * (zz * _sigmoid(zz))).astype(o_ref.dtype)

    def alternate(*gens):
        gens = list(gens)
        while gens:
            for g in list(gens):
                if next(g, StopIteration) is StopIteration:
                    gens.remove(g)

    groups = [list(range(g0, g0 + ua)) for g0 in range(0, tt // c, ua)]
    for ci in groups[0]:
        prepass(ci)
    for gi, grp in enumerate(groups):
        if gi + 1 < len(groups):
            for ci in groups[gi + 1]:
                prepass(ci)
        if gi == 0:
            alternate(phase_a(grp))
        else:
            alternate(phase_a(grp), phase_b(groups[gi - 1]))
    alternate(phase_b(groups[-1]))
    for h in heads:
        s_scr[h] = state[h]

    @pl.when(i == pl.num_programs(1) - 1)
    def _():
        sout_ref[...] = s_scr[...]
        if zero_history:
            tail = qkv_ref[tt - PREV_ROWS:tt, :].astype(F32)
            convout_ref[...] = tail[PREV_ROWS - (CONV_W - 1):PREV_ROWS, :]
        else:
            convout_ref[...] = xbuf[hist + tt - (CONV_W - 1):hist + tt, :]


GDN_PHASE_A_UNROLL = 4
PREV_ROWS = 16
SHIFT_COLS = 512


def _gdn(qkv, z, ab, conv0, s0, conv_w, a_log, dt_bias, norm_w, batch, t):
    c = min(CHUNK, t)
    tt = min(TOKEN_TILE, t)
    nt = t // tt
    n_chunks = tt // c
    ua = min(GDN_PHASE_A_UNROLL, n_chunks)
    zero_history = conv0 is None
    if zero_history:
        assert s0 is None and qkv.dtype == BF16 and tt % PREV_ROWS == 0
        conv0 = jnp.zeros((batch, CONV_W - 1, GDN_CONV_CH), F32)
        s0 = jnp.zeros((batch, GDN_HEADS, GDN_DK, GDN_DV), F32)
    conv0p = jnp.concatenate(
        [jnp.zeros((batch, SUBLANES - (CONV_W - 1), GDN_CONV_CH), F32), conv0.astype(F32)], axis=1)

    def pad_vec(vv):
        return jnp.zeros((1, LANES), F32).at[0, :vv.shape[0]].set(vv.astype(F32))

    tok = lambda b, i: (b * nt + i, 0)
    prev_rows = min(PREV_ROWS, tt)
    per = tt // prev_rows
    o, s_out, conv_out = pl.pallas_call(
        functools.partial(_gdn_kernel, tt=tt, c=c, ua=ua, zero_history=zero_history),
        grid=(batch, nt),
        in_specs=[
            pl.BlockSpec((tt, GDN_CONV_CH), tok),
            pl.BlockSpec((prev_rows, GDN_CONV_CH), lambda b, i: (jnp.maximum((b * nt + i) * per - 1, 0), 0)),
            pl.BlockSpec((tt, GDN_V), tok),
            pl.BlockSpec((tt, AB_PAD), tok),
            pl.BlockSpec((None, SUBLANES, GDN_CONV_CH), lambda b, i: (b, 0, 0)),
            pl.BlockSpec((None, GDN_HEADS, GDN_DK, GDN_DV), lambda b, i: (b, 0, 0, 0)),
            _const_spec((CONV_W, GDN_CONV_CH)),
            _const_spec((1, LANES)),
            _const_spec((1, LANES)),
            _const_spec((1, GDN_DV)),
        ],
        out_specs=[
            pl.BlockSpec((tt, GDN_V), tok),
            pl.BlockSpec((None, GDN_HEADS, GDN_DK, GDN_DV), lambda b, i: (b, 0, 0, 0)),
            pl.BlockSpec((None, CONV_W - 1, GDN_CONV_CH), lambda b, i: (b, 0, 0)),
        ],
        out_shape=[
            jax.ShapeDtypeStruct((batch * t, GDN_V), z.dtype),
            jax.ShapeDtypeStruct((batch, GDN_HEADS, GDN_DK, GDN_DV), F32),
            jax.ShapeDtypeStruct((batch, CONV_W - 1, GDN_CONV_CH), F32),
        ],
        scratch_shapes=[
            pltpu.VMEM((SUBLANES, LANES) if zero_history else (tt + SUBLANES, GDN_CONV_CH), F32),
            pltpu.VMEM((tt, GDN_CONV_CH), F32),
            pltpu.VMEM((tt, LANES), F32),
            pltpu.VMEM((tt, LANES), F32),
            pltpu.VMEM((GDN_HEADS, GDN_DK, GDN_DV), F32),
            pltpu.VMEM((tt, GDN_V), F32),
            pltpu.VMEM((tt, GDN_QK), BF16),
            pltpu.VMEM((tt, GDN_QK), BF16),
            pltpu.VMEM((n_chunks * GDN_HEADS, GDN_DK, c), BF16),
            pltpu.VMEM((n_chunks * GDN_HEADS, c, c), BF16),
            pltpu.VMEM((n_chunks * SUBLANES, LANES), F32),
        ],
        compiler_params=_params(2),
        name="gated_deltanet",
    )(qkv, qkv, z, ab, conv0p, s0.astype(F32), conv_w.astype(F32), pad_vec(a_log), pad_vec(dt_bias),
      norm_w.astype(F32).reshape(1, GDN_DV))
    return o, s_out, conv_out


def _swa_kernel(sink_ref, q_ref, kp_ref, kc_ref, vp_ref, vc_ref, o_ref, kbuf, vbuf, *, tq, cq, prev,
                mask_history):
    i = pl.program_id(1)
    kbuf[0:prev, :] = kp_ref[...]
    kbuf[prev:prev + tq, :] = kc_ref[...]
    vbuf[0:prev, :] = vp_ref[...]
    vbuf[prev:prev + tq, :] = vc_ref[...]
    nk = prev + cq
    group = SWA_HEADS // SWA_KV_HEADS
    lane = lax.broadcasted_iota(jnp.int32, (1, LANES), 1)
    low = lane < SWA_HD
    rowi = lax.broadcasted_iota(jnp.int32, (2 * cq, 1), 0)
    top = rowi < cq
    coli = lax.broadcasted_iota(jnp.int32, (1, nk), 1)
    zero = jnp.zeros((nk, LANES), BF16)

    def scores(ci):
        r0 = ci * cq
        units = []
        for j in range(SWA_KV_HEADS):
            c0 = j * group * SWA_HD
            qst = (jnp.concatenate([q_ref[r0:r0 + cq, c0:c0 + LANES],
                                    q_ref[r0:r0 + cq, c0 + LANES:c0 + 2 * LANES]], axis=0).astype(F32)
                   * SWA_SCALE).astype(BF16)
            kk = kbuf[r0:r0 + nk, j * LANES:(j + 1) * LANES].astype(BF16)
            for half in range(2):
                sel = low if half == 0 else jnp.logical_not(low)
                units.append((j, half, lax.dot_general(qst, jnp.where(sel, kk, zero), (((1,), (1,)), ((), ())),
                                                       preferred_element_type=F32)))
        return ci, units

    def finish(ci, units):
        r0 = ci * cq
        masked = mask_history and ci < WIN_CHUNKS
        if masked:
            n_missing = jnp.maximum(WIN_CHUNKS - (i * (tq // cq) + ci), 0) * CHUNK
            valid = coli >= n_missing
        probs = []
        for j, half, s in units:
            if masked:
                s = jnp.where(valid, s, -jnp.inf)
            sink = jnp.where(top, sink_ref[j * group + half], sink_ref[j * group + 2 + half])
            m = jnp.maximum(jnp.max(s, axis=-1, keepdims=True), sink)
            p = jnp.exp(s - m)
            den = jnp.sum(p, axis=-1, keepdims=True) + jnp.exp(sink - m)
            probs.append((p / den).astype(BF16))
        for j in range(SWA_KV_HEADS):
            c0 = j * group * SWA_HD
            vv = vbuf[r0:r0 + nk, j * LANES:(j + 1) * LANES].astype(BF16)
            acc = (jnp.dot(probs[2 * j], jnp.where(low, vv, zero), preferred_element_type=F32)
                   + jnp.dot(probs[2 * j + 1], jnp.where(low, zero, vv), preferred_element_type=F32))
            o_ref[r0:r0 + cq, c0:c0 + LANES] = acc[:cq].astype(o_ref.dtype)
            o_ref[r0:r0 + cq, c0 + LANES:c0 + 2 * LANES] = acc[cq:].astype(o_ref.dtype)

    pending = scores(0)
    for ci in range(1, tq // cq):
        nxt = scores(ci)
        finish(*pending)
        pending = nxt
    finish(*pending)


def _swa_call(sinks, q, k_prev, k_cur, v_prev, v_cur, prev_spec, batch, t, tq, cq, prev, mask_history):
    nt = t // tq
    tok = lambda b, i: (b * nt + i, 0)
    return pl.pallas_call(
        functools.partial(_swa_kernel, tq=tq, cq=cq, prev=prev, mask_history=mask_history),
        grid=(batch, nt),
        in_specs=[
            pl.BlockSpec(memory_space=pltpu.SMEM),
            pl.BlockSpec((tq, SWA_Q), tok),
            prev_spec,
            pl.BlockSpec((tq, SWA_KV_DUP), tok),
            prev_spec,
            pl.BlockSpec((tq, SWA_KV_DUP), tok),
        ],
        out_specs=pl.BlockSpec((tq, SWA_Q), tok),
        out_shape=jax.ShapeDtypeStruct((batch * t, SWA_Q), q.dtype),
        scratch_shapes=[pltpu.VMEM((prev + tq, SWA_KV_DUP), k_cur.dtype),
                        pltpu.VMEM((prev + tq, SWA_KV_DUP), v_cur.dtype)],
        compiler_params=_params(2),
        name="sliding_window_attention",
    )(sinks.astype(F32), q, k_prev, k_cur, v_prev, v_cur)


def _swa_prompt(sinks, q, k, v, batch, t):
    tq = min(TOKEN_TILE, t)
    nt = t // tq
    per = tq // WINDOW
    prev_spec = pl.BlockSpec((WINDOW, SWA_KV_DUP), lambda b, i: (jnp.maximum((b * nt + i) * per - 1, 0), 0))
    return _swa_call(sinks, q, k, k, v, v, prev_spec, batch, t, tq, CHUNK, WINDOW, True)


def _swa_sample(sinks, q, k, v, cache_k, cache_v, batch, t):
    n_keep = cache_k.shape[1]
    q_pos = PAST_LEN + np.arange(t)
    k_pos = np.concatenate([PAST_LEN - n_keep + np.arange(n_keep), q_pos])
    qc = (q_pos // CHUNK)[:, None]
    kc = (k_pos // CHUNK)[None, :]
    assert np.all((kc <= qc) & (kc >= qc - WIN_CHUNKS)), "sample step expects every cached row in window"
    prev_spec = pl.BlockSpec((n_keep, SWA_KV_DUP), lambda b, i: (b, 0))
    return _swa_call(sinks, q, cache_k.reshape(batch * n_keep, SWA_KV_DUP), k,
                     cache_v.reshape(batch * n_keep, SWA_KV_DUP), v, prev_spec, batch, t, t, t, n_keep, False)


def _dup_heads(t):
    return jnp.concatenate([t[..., 0, :], t[..., 0, :], t[..., 1, :], t[..., 1, :]], axis=-1)


def _undup_heads(t):
    return jnp.stack([t[:, :SWA_HD], t[:, 2 * SWA_HD:3 * SWA_HD]], axis=1).astype(F32)


def _memattn_kernel(q_ref, mk_ref, mv_ref, o_ref, *, rows):
    tt = q_ref.shape[0]

    def scores(h, r0):
        ls = slice(h * XA_HD, (h + 1) * XA_HD)
        s = lax.dot_general(q_ref[r0:r0 + rows, ls].astype(BF16), mk_ref[:, ls].astype(BF16),
                            (((1,), (1,)), ((), ())), preferred_element_type=F32)
        return h, r0, s

    def finish(h, r0, s):
        ls = slice(h * XA_HD, (h + 1) * XA_HD)
        s = s * (XA_HD ** -0.5)
        m = jnp.max(s, axis=-1, keepdims=True)
        p = jnp.exp(s - m)
        p = (p / jnp.sum(p, axis=-1, keepdims=True)).astype(BF16)
        o_ref[r0:r0 + rows, ls] = jnp.dot(p, mv_ref[:, ls].astype(BF16),
                                          preferred_element_type=F32).astype(o_ref.dtype)

    blocks = [(h, r0) for h in range(XA_HEADS) for r0 in range(0, tt, rows)]
    pending = scores(*blocks[0])
    for blk in blocks[1:]:
        nxt = scores(*blk)
        finish(*pending)
        pending = nxt
    finish(*pending)


def _mem_attention(q, mk, mv, batch, t):
    tt = min(TOKEN_TILE, t)
    nt = t // tt
    tok = lambda b, i: (b * nt + i, 0)
    mem = pl.BlockSpec((None, N_MEM, XA_Q), lambda b, i: (b, 0, 0))
    return pl.pallas_call(
        functools.partial(_memattn_kernel, rows=min(256, tt)),
        grid=(batch, nt),
        in_specs=[pl.BlockSpec((tt, XA_Q), tok), mem, mem],
        out_specs=pl.BlockSpec((tt, XA_Q), tok),
        out_shape=jax.ShapeDtypeStruct((batch * t, XA_Q), q.dtype),
        compiler_params=_params(2),
        name="memory_attention",
    )(q, mk, mv)


MERGE_ROWS = 256


def _merge_kernel(x_ref, og_ref, os_ref, ox_ref, gate_ref, wb_ref, wo_ref, g_ref, b_ref, out_ref):
    rows = min(MERGE_ROWS, x_ref.shape[0])

    def branches(r0):
        rs = slice(r0, r0 + rows)
        return r0, [jnp.dot(ref[rs, :].astype(BF16), wb_ref[bi], preferred_element_type=F32)
                    for bi, ref in enumerate((og_ref, os_ref, ox_ref))]

    def finish(r0, ys):
        rs = slice(r0, r0 + rows)
        merged = None
        for bi, y in enumerate(ys):
            y = gate_ref[rs, bi * D_MODEL:(bi + 1) * D_MODEL].astype(F32) * y
            merged = y if merged is None else merged + y
        u = jnp.dot(merged.astype(BF16), wo_ref[...], preferred_element_type=F32)
        out_ref[rs, :] = _layer_norm(DN_ALPHA * x_ref[rs, :] + u, g_ref[...], b_ref[...])

    pending = branches(0)
    for r0 in range(rows, x_ref.shape[0], rows):
        nxt = branches(r0)
        finish(*pending)
        pending = nxt
    finish(*pending)


def _merge(x2d, o_g, o_s, o_x, gate, w_branch, w_o, layer, ln_g, ln_b):
    m = x2d.shape[0]
    tm = min(TOKEN_TILE, m)
    row = lambda w: pl.BlockSpec((tm, w), lambda i: (i, 0))
    return pl.pallas_call(
        _merge_kernel,
        grid=(m // tm,),
        in_specs=[row(D_MODEL), row(BRANCH_W), row(BRANCH_W), row(BRANCH_W), row(N_BRANCH * D_MODEL),
                  _layer_spec((N_BRANCH, BRANCH_W, D_MODEL), layer), _layer_spec((D_MODEL, D_MODEL), layer),
                  _const_spec((1, D_MODEL)), _const_spec((1, D_MODEL))],
        out_specs=row(D_MODEL),
        out_shape=jax.ShapeDtypeStruct((m, D_MODEL), F32),
        compiler_params=_params(1),
        name="branch_merge",
    )(x2d, o_g, o_s, o_x, gate, w_branch, w_o, ln_g.reshape(1, D_MODEL), ln_b.reshape(1, D_MODEL))


FF_CHUNK = 1024


def _ffn_kernel(x_ref, wu_ref, wd_ref, g_ref, b_ref, out_ref):
    x = x_ref[...]
    xb = x.astype(BF16)
    def up(c0):
        return c0, jnp.dot(xb, wu_ref[:, c0:c0 + FF_CHUNK], preferred_element_type=F32)

    def down(c0, hid):
        hid = jnp.square(jnp.maximum(hid, 0.0)).astype(BF16)
        return jnp.dot(hid, wd_ref[c0:c0 + FF_CHUNK, :], preferred_element_type=F32)

    pending = up(0)
    acc = None
    for c0 in range(FF_CHUNK, D_FF, FF_CHUNK):
        nxt = up(c0)
        part = down(*pending)
        acc = part if acc is None else acc + part
        pending = nxt
    part = down(*pending)
    acc = part if acc is None else acc + part
    out_ref[...] = _layer_norm(DN_ALPHA * x + acc, g_ref[...], b_ref[...])


def _ffn(x2d, w_up, w_down, layer, ln_g, ln_b):
    m = x2d.shape[0]
    tm = min(TOKEN_TILE, m)
    row = pl.BlockSpec((tm, D_MODEL), lambda i: (i, 0))
    return pl.pallas_call(
        _ffn_kernel,
        grid=(m // tm,),
        in_specs=[row, _layer_spec((D_MODEL, D_FF), layer), _layer_spec((D_FF, D_MODEL), layer),
                  _const_spec((1, D_MODEL)), _const_spec((1, D_MODEL))],
        out_specs=row,
        out_shape=jax.ShapeDtypeStruct((m, D_MODEL), F32),
        compiler_params=_params(1),
        name="channel_mixer",
    )(x2d, w_up, w_down, ln_g.reshape(1, D_MODEL), ln_b.reshape(1, D_MODEL))


def _layer(x2d, batch, t, act_dtype, lw, conv0, s0, mk, mv, swa_cache):
    h = _in_projection(x2d, lw["w_in"], lw["layer"], act_dtype)
    o_g, s_new, conv_new = _gdn(h["qkv"], h["z"], h["ab"], conv0, s0, lw["conv_w"], lw["a_log"], lw["dt_bias"],
                                lw["gdn_norm_w"], batch, t)
    if swa_cache is None:
        o_s = _swa_prompt(lw["attn_sinks"], h["qs"], h["ks"], h["vs"], batch, t)
    else:
        o_s = _swa_sample(lw["attn_sinks"], h["qs"], h["ks"], h["vs"], swa_cache[0], swa_cache[1], batch, t)
    o_x = _mem_attention(h["qx"], mk, mv, batch, t)
    x1 = _merge(x2d, o_g, o_s, o_x, h["gate"], lw["w_branch"], lw["w_o"], lw["layer"], lw["ln1_g"], lw["ln1_b"])
    x2 = _ffn(x1, lw["w_up"], lw["w_down"], lw["layer"], lw["ln2_g"], lw["ln2_b"])
    return x2, s_new, conv_new, h["ks"], h["vs"]


def kernel(x_prompt, x_sample, state_gdn_s, state_gdn_conv, cache_swa_k, cache_swa_v, cache_mem_k, cache_mem_v,
           mem_prompt, w_in, conv_w, a_log, dt_bias, gdn_norm_w, attn_sinks, w_mem_kv, w_branch, w_o,
           ln1_g, ln1_b, w_up, w_down, ln2_g, ln2_b):
    depth = w_in.shape[0]
    stacked = dict(w_in=_prep_w_in(w_in), w_mem_kv=w_mem_kv.astype(BF16), w_branch=w_branch.astype(BF16),
                   w_o=w_o.astype(BF16), w_up=w_up.astype(BF16), w_down=w_down.astype(BF16))
    layers = []
    for l in range(depth):
        layers.append(dict(
            stacked, layer=l, conv_w=conv_w[l], a_log=a_log[l], dt_bias=dt_bias[l],
            gdn_norm_w=gdn_norm_w[l], attn_sinks=attn_sinks[l], ln1_g=ln1_g[l], ln1_b=ln1_b[l],
            ln2_g=ln2_g[l], ln2_b=ln2_b[l]))

    bp, tp, _ = x_prompt.shape
    n_keep = cache_swa_k.shape[2]
    x = x_prompt.reshape(bp * tp, D_MODEL)
    mem2d = mem_prompt.reshape(bp * N_MEM, D_MODEL)
    p_s, p_conv, p_k, p_v, p_mk, p_mv = [], [], [], [], [], []
    for lw in layers:
        mkv = _matmul(mem2d, lw["w_mem_kv"], lw["layer"]).reshape(bp, N_MEM, 2 * XA_Q)
        mk, mv = mkv[..., :XA_Q], mkv[..., XA_Q:]
        x, s_new, conv_new, ks, vs = _layer(x, bp, tp, BF16, lw, None, None, mk, mv, None)
        p_s.append(s_new)
        p_conv.append(conv_new)
        p_k.append(_undup_heads(ks.reshape(bp, tp, SWA_KV_DUP)[:, tp - n_keep:].reshape(bp * n_keep, SWA_KV_DUP))
                   .reshape(bp, n_keep, SWA_KV_HEADS, SWA_HD))
        p_v.append(_undup_heads(vs.reshape(bp, tp, SWA_KV_DUP)[:, tp - n_keep:].reshape(bp * n_keep, SWA_KV_DUP))
                   .reshape(bp, n_keep, SWA_KV_HEADS, SWA_HD))
        p_mk.append(mk.reshape(bp, N_MEM, XA_HEADS, XA_HD))
        p_mv.append(mv.reshape(bp, N_MEM, XA_HEADS, XA_HD))
    y_prompt = x.reshape(bp, tp, D_MODEL)

    bs, ts, _ = x_sample.shape
    x = x_sample.reshape(bs * ts, D_MODEL)
    s_s, s_conv, s_k, s_v = [], [], [], []
    for l, lw in enumerate(layers):
        cache = (_dup_heads(cache_swa_k[l]), _dup_heads(cache_swa_v[l]))
        x, s_new, conv_new, ks, vs = _layer(
            x, bs, ts, F32, lw, state_gdn_conv[l], state_gdn_s[l],
            cache_mem_k[l].reshape(bs, N_MEM, XA_Q), cache_mem_v[l].reshape(bs, N_MEM, XA_Q), cache)
        s_s.append(s_new)
        s_conv.append(conv_new)
        k_new = _undup_heads(ks).reshape(bs, ts, SWA_KV_HEADS, SWA_HD)
        v_new = _undup_heads(vs).reshape(bs, ts, SWA_KV_HEADS, SWA_HD)
        s_k.append(jnp.concatenate([cache_swa_k[l], k_new], axis=1)[:, -n_keep:])
        s_v.append(jnp.concatenate([cache_swa_v[l], v_new], axis=1)[:, -n_keep:])
    y_sample = x.reshape(bs, ts, D_MODEL)

    return (y_prompt, y_sample,
            jnp.stack(p_s), jnp.stack(p_conv), jnp.stack(p_k), jnp.stack(p_v), jnp.stack(p_mk), jnp.stack(p_mv),
            jnp.stack(s_s), jnp.stack(s_conv), jnp.stack(s_k), jnp.stack(s_v))
```

```python
import functools

import jax
import jax.numpy as jnp
import numpy as np
from jax import lax
from jax.experimental import pallas as pl
from jax.experimental.pallas import tpu as pltpu

F32 = jnp.float32
BF16 = jnp.bfloat16
HI = lax.Precision.HIGHEST

D_MODEL = 1024
DEPTH = 2
CHUNK = 64
PAST_LEN = 2048
GDN_HEADS = 4
GDN_DK = 128
GDN_DV = 128
CONV_W = 4
SWA_HEADS = 8
SWA_KV_HEADS = 2
SWA_HD = 64
WINDOW = 128
WIN_CHUNKS = WINDOW // CHUNK
SWA_SCALE = SWA_HD ** -0.5
assert np.log2(SWA_SCALE) == round(np.log2(SWA_SCALE)), "the scale is folded into q before a bf16 cast"
N_MEM = 256
XA_HEADS = 4
XA_HD = 128
D_FF = 4 * D_MODEL
N_BRANCH = 3
BRANCH_W = 512
DN_ALPHA = (2.0 * DEPTH) ** 0.25
LN_EPS = 1e-5
NORM_EPS = 1e-6
GDN_QK = GDN_HEADS * GDN_DK
GDN_V = GDN_HEADS * GDN_DV
GDN_CONV_CH = 2 * GDN_QK + GDN_V
SWA_Q = SWA_HEADS * SWA_HD
SWA_KV = SWA_KV_HEADS * SWA_HD
XA_Q = XA_HEADS * XA_HD
IN_SPLITS = (GDN_CONV_CH, GDN_HEADS, GDN_HEADS, GDN_V, SWA_Q, SWA_KV, SWA_KV, XA_Q, N_BRANCH * D_MODEL)

LANES = 128
SUBLANES = 8
TOKEN_TILE = 512
VMEM_LIMIT = 56 * 2**20

SWA_KV_DUP = 2 * SWA_KV
AB_PAD = LANES
PROJ_OUT = (("qkv", GDN_CONV_CH), ("z", GDN_V), ("qs", SWA_Q), ("qx", XA_Q), ("ks", SWA_KV_DUP),
            ("vs", SWA_KV_DUP), ("ab", AB_PAD), ("gate", N_BRANCH * D_MODEL))
PROJ_COLS = sum(w for _, w in PROJ_OUT)
PROJ_CHUNK = 512
CONV_CHUNK = 256


def _params(n_grid):
    return pltpu.CompilerParams(dimension_semantics=("arbitrary",) * n_grid, vmem_limit_bytes=VMEM_LIMIT)


def _const_spec(shape):
    zeros = (0,) * len(shape)
    return pl.BlockSpec(shape, lambda *_: zeros, pipeline_mode=pl.Buffered(1))


def _layer_spec(shape, layer):
    zeros = (0,) * len(shape)
    return pl.BlockSpec((None,) + tuple(shape), lambda *_: (layer,) + zeros, pipeline_mode=pl.Buffered(1))


def _sigmoid(x):
    return 1.0 / (1.0 + jnp.exp(-x))


def _softplus(x):
    return jnp.maximum(x, 0.0) + jnp.log1p(jnp.exp(-jnp.abs(x)))


def _layer_norm(r, g, b):
    mu = jnp.mean(r, axis=-1, keepdims=True)
    d = r - mu
    var = jnp.mean(d * d, axis=-1, keepdims=True)
    return d * lax.rsqrt(var + LN_EPS) * g + b


CONV_HIST = SUBLANES
CONV_ROWS = 128


def _conv_silu_norm(cw_ref, buf, r0, rows, cg, buf_lane0=0):
    ls = slice(cg * LANES, (cg + 1) * LANES)
    bl = slice(cg * LANES - buf_lane0, (cg + 1) * LANES - buf_lane0)
    acc = None
    for w in range(CONV_W):
        s = CONV_HIST - (CONV_W - 1) + w + r0
        term = cw_ref[w:w + 1, ls] * buf[s:s + rows, bl]
        acc = term if acc is None else acc + term
    y = acc * _sigmoid(acc)
    if cg < 2 * GDN_HEADS:
        inv = lax.rsqrt(jnp.sum(y * y, axis=-1, keepdims=True) + NORM_EPS)
        if cg < GDN_HEADS:
            inv = inv * (GDN_DK ** -0.5)
        y = y * inv
    return y


def _inproj_kernel(*refs, tiles_per_stream):
    fuse_conv = tiles_per_stream is not None
    n_out = len(PROJ_OUT)
    if fuse_conv:
        x_ref, w_ref, cw_ref = refs[:3]
        out_refs, tail_ref, ybufs, xb_s = refs[3:3 + n_out], refs[3 + n_out], refs[4 + n_out:-1], refs[-1]

        @pl.when(pl.program_id(0) % tiles_per_stream == 0)
        def _():
            for ybuf in ybufs:
                ybuf[0:CONV_HIST, :] = jnp.zeros((CONV_HIST, CONV_CHUNK), F32)
    else:
        x_ref, w_ref = refs[:2]
        out_refs, xb_s = refs[2:-1], refs[-1]
    tm = x_ref.shape[0]
    xb_s[...] = x_ref[...].astype(BF16)
    chunks = []
    off = 0
    for (name, width), ref in zip(PROJ_OUT, out_refs):
        step = CONV_CHUNK if (fuse_conv and name == "qkv") else PROJ_CHUNK
        chunks += [(name, ref, off, c0, min(step, width - c0)) for c0 in range(0, width, step)]
        off += width
    if fuse_conv:
        pre = [ch for ch in chunks if ch[0] == "qkv"]
        rest = [ch for ch in chunks if ch[0] != "qkv"]
        per = -(-len(rest) // len(pre))
        chunks = []
        for n, ch in enumerate(pre):
            chunks += [ch] + rest[n * per:(n + 1) * per]
    pending = []
    for n, (name, ref, off, c0, cw) in enumerate(chunks):
        y = jnp.dot(xb_s[...], w_ref[:, off + c0:off + c0 + cw], preferred_element_type=F32)
        if name == "qkv" and fuse_conv:
            ybuf = ybufs[c0 // CONV_CHUNK]
            ybuf[CONV_HIST:CONV_HIST + tm, :] = y
            tail_ref[:, c0:c0 + cw] = ybuf[tm:tm + CONV_HIST, :]
            pending = [(ybuf, c0, cg, r0) for cg in range(c0 // LANES, (c0 + cw) // LANES)
                       for r0 in range(0, tm, CONV_ROWS)]
            n_follow = len([ch for ch in chunks[n + 1:n + 1 + per] if ch[0] != "qkv"])
            share = -(-len(pending) // max(n_follow, 1))
            continue
        if name == "gate":
            y = _sigmoid(y)
        ref[:, c0:c0 + cw] = y.astype(ref.dtype)
        for ybuf, lane0, cg, r0 in pending[:share] if fuse_conv else []:
            out_refs[0][r0:r0 + CONV_ROWS, cg * LANES:(cg + 1) * LANES] = _conv_silu_norm(
                cw_ref, ybuf, r0, CONV_ROWS, cg, lane0).astype(out_refs[0].dtype)
        pending = pending[share:] if fuse_conv else pending
    assert not pending
    if fuse_conv:
        for ybuf in ybufs:
            ybuf[0:CONV_HIST, :] = ybuf[tm:tm + CONV_HIST, :]


def _in_projection(x2d, w_all, layer, act_dtype, conv_w=None, stream_len=None):
    m = x2d.shape[0]
    tm = min(TOKEN_TILE, m)
    fuse_conv = conv_w is not None
    dtypes = {"ab": F32}
    out_shape = [jax.ShapeDtypeStruct((m, w), dtypes.get(n, act_dtype)) for n, w in PROJ_OUT]
    out_specs = [pl.BlockSpec((tm, w), lambda i: (i, 0)) for _, w in PROJ_OUT]
    in_specs = [pl.BlockSpec((tm, D_MODEL), lambda i: (i, 0)), _layer_spec((D_MODEL, PROJ_COLS), layer)]
    args = [x2d, w_all]
    names = [n for n, _ in PROJ_OUT]
    scratch = []
    if fuse_conv:
        assert stream_len % tm == 0 and tm % CONV_ROWS == 0
        in_specs.append(_const_spec((CONV_W, GDN_CONV_CH)))
        args.append(conv_w.astype(F32))
        out_shape.append(jax.ShapeDtypeStruct((m // tm * CONV_HIST, GDN_CONV_CH), F32))
        out_specs.append(pl.BlockSpec((CONV_HIST, GDN_CONV_CH), lambda i: (i, 0)))
        names.append("conv_tail")
        assert GDN_CONV_CH % CONV_CHUNK == 0
        scratch += [pltpu.VMEM((CONV_HIST + tm, CONV_CHUNK), F32) for _ in range(GDN_CONV_CH // CONV_CHUNK)]
    scratch.append(pltpu.VMEM((tm, D_MODEL), BF16))
    outs = pl.pallas_call(
        functools.partial(_inproj_kernel, tiles_per_stream=stream_len // tm if fuse_conv else None),
        grid=(m // tm,),
        in_specs=in_specs,
        out_specs=out_specs,
        out_shape=out_shape,
        scratch_shapes=scratch,
        compiler_params=_params(1),
        name="in_projection",
    )(*args)
    return dict(zip(names, outs))


def _prep_w_in(w):
    offs = np.cumsum((0,) + IN_SPLITS)
    qkv, a, b, z, qs, ks, vs, qx, gate = [w[..., offs[i]:offs[i + 1]] for i in range(len(IN_SPLITS))]

    def dup(t):
        return [t[..., :SWA_HD], t[..., :SWA_HD], t[..., SWA_HD:], t[..., SWA_HD:]]

    pad = jnp.zeros(w.shape[:-1] + (AB_PAD - 2 * GDN_HEADS,), w.dtype)
    return jnp.concatenate([qkv, z, qs, qx, *dup(ks), *dup(vs), a, b, pad, gate], axis=-1).astype(BF16)


def _matmul_kernel(x_ref, w_ref, o_ref):
    o_ref[...] = jnp.dot(x_ref[...].astype(BF16), w_ref[...], preferred_element_type=F32)


def _matmul(x2d, w_bf16, layer):
    m, k = x2d.shape
    n = w_bf16.shape[-1]
    tm = min(TOKEN_TILE, m)
    return pl.pallas_call(
        _matmul_kernel,
        grid=(m // tm,),
        in_specs=[pl.BlockSpec((tm, k), lambda i: (i, 0)), _layer_spec((k, n), layer)],
        out_specs=pl.BlockSpec((tm, n), lambda i: (i, 0)),
        out_shape=jax.ShapeDtypeStruct((m, n), F32),
        compiler_params=_params(1),
        name="mem_kv_projection",
    )(x2d, w_bf16)


def _gdn_kernel(*refs, tt, c, ua, conv_done):
    refs = list(refs)
    if conv_done:
        (qkv_ref, z_ref, ab_ref, s0_ref, alog_ref, dtb_ref, nw_ref, o_ref, sout_ref,
         gbuf, bbuf, s_scr, u_s, w_s, qe_s, kdt_s, qk_s, el_s) = refs
    else:
        (qkv_ref, z_ref, ab_ref, conv0_ref, s0_ref, cw_ref, alog_ref, dtb_ref, nw_ref, o_ref, sout_ref, convout_ref,
         xbuf, cbuf, gbuf, bbuf, s_scr, u_s, w_s, qe_s, kdt_s, qk_s, el_s) = refs
    i = pl.program_id(1)

    @pl.when(i == 0)
    def _():
        s_scr[...] = s0_ref[...]

    if not conv_done:
        @pl.when(i == 0)
        def _():
            xbuf[0:CONV_HIST, :] = conv0_ref[...]

        @pl.when(i > 0)
        def _():
            xbuf[0:CONV_HIST, :] = xbuf[tt:tt + CONV_HIST, :]

        xbuf[CONV_HIST:CONV_HIST + tt, :] = qkv_ref[...].astype(F32)

    def conv_out(rows, cols):
        return qkv_ref[rows, cols].astype(F32) if conv_done else cbuf[rows, cols]

    expalog = jnp.exp(alog_ref[...])
    dtb = dtb_ref[...]

    def prepass(ci):
        r0 = ci * c
        if not conv_done:
            for cg in range(GDN_CONV_CH // LANES):
                cbuf[r0:r0 + c, cg * LANES:(cg + 1) * LANES] = _conv_silu_norm(cw_ref, xbuf, r0, c, cg)
        ab = ab_ref[r0:r0 + c, :]
        gbuf[r0:r0 + c, :] = -expalog * _softplus(ab + dtb)
        bbuf[r0:r0 + c, :] = _sigmoid(ab)

    pairs = GDN_HEADS // 2
    row = lax.broadcasted_iota(jnp.int32, (c, 2 * c), 0)
    lane = lax.broadcasted_iota(jnp.int32, (c, 2 * c), 1)
    left = lane < c
    col = jnp.where(left, lane, lane - c)
    causal = row >= col
    strict = row > col
    l_strict = strict.astype(F32)
    l_incl2 = causal.astype(BF16)
    n_square = max(int(np.ceil(np.log2(c))) - 1, 0)
    nw = nw_ref[...]
    nt_dims = (((1,), (1,)), ((), ()))

    def split2(x):
        hi = x.astype(BF16)
        lo = (x - hi.astype(F32)).astype(BF16)
        return hi, lo

    def block_diag(a, b):
        zero = jnp.zeros_like(a)
        return jnp.concatenate([jnp.concatenate([a, zero], axis=1), jnp.concatenate([zero, b], axis=1)], axis=0)

    def pair_diag(p):
        zero = jnp.zeros_like(p)
        return jnp.concatenate([jnp.where(left, p, zero), jnp.where(left, zero, p)], axis=0)

    def phase_a(chunk_ids):
        chunks = []
        for ci in chunk_ids:
            rows = slice(ci * c, (ci + 1) * c)
            g_hi, g_lo = split2(gbuf[rows, :])
            gc_all = jnp.dot(l_incl2, jnp.concatenate([g_hi, g_lo], axis=0), preferred_element_type=F32)
            chunks.append((ci, rows, g_hi.astype(F32), g_lo.astype(F32), gc_all))
        yield
        chains = []
        for ci, rows, g_hi, g_lo, gc_all in chunks:
            el_s[ci * SUBLANES:(ci + 1) * SUBLANES, :] = jnp.broadcast_to(jnp.exp(gc_all[c - 1:c, :]),
                                                                         (SUBLANES, LANES))
            b_all = bbuf[rows, :]
            for hp in range(pairs):
                ha, hb = 2 * hp, 2 * hp + 1
                beta = [b_all[:, GDN_HEADS + h:GDN_HEADS + h + 1] for h in (ha, hb)]
                steps = jnp.concatenate(
                    [(jnp.where(left, g[:, ha:ha + 1], g[:, hb:hb + 1]) * l_strict).astype(BF16)
                     for g in (g_hi, g_lo)], axis=0)
                diff = jnp.dot(l_incl2, steps, preferred_element_type=F32)
                q2 = conv_out(rows, slice(ha * GDN_DK, (hb + 1) * GDN_DK))
                k = [conv_out(rows, slice(GDN_QK + h * GDN_DK, GDN_QK + (h + 1) * GDN_DK)) for h in (ha, hb)]
                kb2 = jnp.concatenate([k[0] * beta[0], k[1] * beta[1]], axis=1)
                gram = lax.dot_general(jnp.concatenate([kb2.astype(BF16), q2.astype(BF16)], axis=0),
                                       block_diag(k[0].astype(BF16), k[1].astype(BF16)), nt_dims,
                                       preferred_element_type=F32)
                chains.append(dict(ci=ci, rows=rows, hp=hp, heads=(ha, hb), beta=beta, diff=diff, gram=gram,
                                   gc=[gc_all[:, h:h + 1] for h in (ha, hb)],
                                   g_last=[gc_all[c - 1:c, h:h + 1] for h in (ha, hb)]))
        yield
        for ch in chains:
            decay = jnp.where(causal, jnp.exp(ch.pop("diff")), 0.0)
            gram = ch.pop("gram")
            qk_s[ch["ci"] * pairs + ch["hp"]] = (gram[c:] * decay).astype(BF16)
            ch["toff"] = -jnp.where(strict, gram[:c] * decay, 0.0)
            ch["pw"] = ch["toff"].astype(BF16)
        for _ in range(n_square):
            for ch in chains:
                ch["pw"] = jnp.dot(ch["pw"], pair_diag(ch["pw"]), preferred_element_type=F32)
            yield
            for ch in chains:
                pwb = ch["pw"].astype(BF16)
                ch["toff"] = ch["toff"] + ch["pw"] + jnp.dot(ch["toff"].astype(BF16), pair_diag(pwb),
                                                             preferred_element_type=F32)
                ch["pw"] = pwb
            yield
        for ch in chains:
            rows = ch["rows"]
            rhs, kdec = [], []
            for n, h in enumerate(ch["heads"]):
                egc = jnp.exp(ch["gc"][n])
                k = conv_out(rows, slice(GDN_QK + h * GDN_DK, GDN_QK + (h + 1) * GDN_DK))
                v = conv_out(rows, slice(2 * GDN_QK + h * GDN_DV, 2 * GDN_QK + (h + 1) * GDN_DV))
                rhs.append(jnp.concatenate([v * ch["beta"][n], k * (ch["beta"][n] * egc)], axis=1))
                kdec.append(k * jnp.exp(ch["g_last"][n] - ch["gc"][n]))
                qe_s[rows, h * GDN_DK:(h + 1) * GDN_DK] = (conv_out(rows, slice(h * GDN_DK, (h + 1) * GDN_DK))
                                                           * egc).astype(BF16)
            uw = jnp.dot(ch["toff"].astype(BF16), block_diag(rhs[0].astype(BF16), rhs[1].astype(BF16)),
                         preferred_element_type=F32)
            for n, h in enumerate(ch["heads"]):
                hq = slice(h * GDN_DK, (h + 1) * GDN_DK)
                uw_h = rhs[n] + uw[:, n * (GDN_DV + GDN_DK):(n + 1) * (GDN_DV + GDN_DK)]
                u_s[rows, hq] = uw_h[:, :GDN_DV]
                w_s[rows, hq] = uw_h[:, GDN_DV:].astype(BF16)
            kdt_s[ch["ci"] * pairs + ch["hp"]] = jnp.concatenate(kdec, axis=0).T.astype(BF16)
        yield

    heads = range(GDN_HEADS)
    hqs = [slice(h * GDN_DK, (h + 1) * GDN_DK) for h in heads]
    state = [s_scr[h] for h in heads]

    def phase_b(chunk_ids):
        for ci in chunk_ids:
            rows = slice(ci * c, (ci + 1) * c)
            e_last = el_s[ci * SUBLANES:ci * SUBLANES + 1, :]
            ws = [jnp.dot(jnp.concatenate([w_s[rows, hqs[h]], qe_s[rows, hqs[h]]], axis=0),
                          state[h].astype(BF16), preferred_element_type=F32) for h in heads]
            yield
            v_new = [(u_s[rows, hqs[h]] - ws[h][:c]).astype(BF16) for h in heads]
            v_diag = [block_diag(v_new[2 * hp], v_new[2 * hp + 1]) for hp in range(pairs)]
            s_inc = [jnp.dot(kdt_s[ci * pairs + hp], v_diag[hp], preferred_element_type=F32)
                     for hp in range(pairs)]
            o_intra = [jnp.dot(qk_s[ci * pairs + hp], v_diag[hp], preferred_element_type=F32)
                       for hp in range(pairs)]
            yield
            for h in heads:
                hp, side = divmod(h, 2)
                lanes = slice(side * GDN_DV, (side + 1) * GDN_DV)
                state[h] = state[h] * e_last[:, h:h + 1] + s_inc[hp][:, lanes]
                o = ws[h][c:] + o_intra[hp][:, lanes]
                o = o * lax.rsqrt(jnp.mean(o * o, axis=-1, keepdims=True) + NORM_EPS) * nw
                zz = z_ref[rows, hqs[h]].astype(F32)
                o_ref[rows, hqs[h]] = (o * (zz * _sigmoid(zz))).astype(o_ref.dtype)

    def alternate(*gens):
        gens = list(gens)
        while gens:
            for g in list(gens):
                if next(g, StopIteration) is StopIteration:
                    gens.remove(g)

    groups = [list(range(g0, g0 + ua)) for g0 in range(0, tt // c, ua)]
    def prepass_chunks(chunk_ids):
        for ci in chunk_ids:
            prepass(ci)
            yield

    alternate(prepass_chunks(groups[0]))
    for gi, grp in enumerate(groups):
        gens = [phase_a(grp)]
        if gi > 0:
            gens.append(phase_b(groups[gi - 1]))
        if gi + 1 < len(groups):
            gens.append(prepass_chunks(groups[gi + 1]))
        alternate(*gens)
    alternate(phase_b(groups[-1]))
    for h in heads:
        s_scr[h] = state[h]

    @pl.when(i == pl.num_programs(1) - 1)
    def _():
        sout_ref[...] = s_scr[...]
        if not conv_done:
            convout_ref[...] = xbuf[CONV_HIST + tt - (CONV_W - 1):CONV_HIST + tt, :]


GDN_PHASE_A_UNROLL = 4
GDN_TOKEN_TILE = 1024


def _gdn(qkv, z, ab, conv0, s0, conv_w, a_log, dt_bias, norm_w, batch, t):
    c = min(CHUNK, t)
    tt = min(GDN_TOKEN_TILE, t)
    nt = t // tt
    n_chunks = tt // c
    ua = min(GDN_PHASE_A_UNROLL, n_chunks)
    conv_done = conv0 is None

    def pad_vec(vv):
        return jnp.zeros((1, LANES), F32).at[0, :vv.shape[0]].set(vv.astype(F32))

    tok = lambda b, i: (b * nt + i, 0)
    state_spec = pl.BlockSpec((None, GDN_HEADS, GDN_DK, GDN_DV), lambda b, i: (b, 0, 0, 0))
    in_specs = [pl.BlockSpec((tt, GDN_CONV_CH), tok), pl.BlockSpec((tt, GDN_V), tok), pl.BlockSpec((tt, AB_PAD), tok)]
    args = [qkv, z, ab]
    if not conv_done:
        in_specs.append(pl.BlockSpec((None, CONV_HIST, GDN_CONV_CH), lambda b, i: (b, 0, 0)))
        args.append(jnp.concatenate([jnp.zeros((batch, CONV_HIST - (CONV_W - 1), GDN_CONV_CH), F32),
                                     conv0.astype(F32)], axis=1))
    in_specs.append(state_spec)
    args.append(s0.astype(F32))
    if not conv_done:
        in_specs.append(_const_spec((CONV_W, GDN_CONV_CH)))
        args.append(conv_w.astype(F32))
    in_specs += [_const_spec((1, LANES)), _const_spec((1, LANES)), _const_spec((1, GDN_DV))]
    args += [pad_vec(a_log), pad_vec(dt_bias), norm_w.astype(F32).reshape(1, GDN_DV)]
    out_specs = [pl.BlockSpec((tt, GDN_V), tok), state_spec]
    out_shape = [jax.ShapeDtypeStruct((batch * t, GDN_V), z.dtype),
                 jax.ShapeDtypeStruct((batch, GDN_HEADS, GDN_DK, GDN_DV), F32)]
    scratch = []
    if not conv_done:
        out_specs.append(pl.BlockSpec((None, CONV_W - 1, GDN_CONV_CH), lambda b, i: (b, 0, 0)))
        out_shape.append(jax.ShapeDtypeStruct((batch, CONV_W - 1, GDN_CONV_CH), F32))
        scratch += [pltpu.VMEM((tt + CONV_HIST, GDN_CONV_CH), F32),
                    pltpu.VMEM((tt, GDN_CONV_CH), F32)]
    scratch += [
        pltpu.VMEM((tt, LANES), F32),
        pltpu.VMEM((tt, LANES), F32),
        pltpu.VMEM((GDN_HEADS, GDN_DK, GDN_DV), F32),
        pltpu.VMEM((tt, GDN_V), F32),
        pltpu.VMEM((tt, GDN_QK), BF16),
        pltpu.VMEM((tt, GDN_QK), BF16),
        pltpu.VMEM((n_chunks * GDN_HEADS // 2, GDN_DK, 2 * c), BF16),
        pltpu.VMEM((n_chunks * GDN_HEADS // 2, c, 2 * c), BF16),
        pltpu.VMEM((n_chunks * SUBLANES, LANES), F32),
    ]
    outs = pl.pallas_call(
        functools.partial(_gdn_kernel, tt=tt, c=c, ua=ua, conv_done=conv_done),
        grid=(batch, nt),
        in_specs=in_specs,
        out_specs=out_specs,
        out_shape=out_shape,
        scratch_shapes=scratch,
        compiler_params=_params(2),
        name="gated_deltanet",
    )(*args)
    return outs[0], outs[1], (None if conv_done else outs[2])


def _swa_kernel(sink_ref, q_ref, kp_ref, kc_ref, vp_ref, vc_ref, o_ref, kbuf, vbuf, *, tq, cq, prev,
                mask_history):
    i = pl.program_id(1)
    kbuf[0:prev, :] = kp_ref[...]
    kbuf[prev:prev + tq, :] = kc_ref[...]
    vbuf[0:prev, :] = vp_ref[...]
    vbuf[prev:prev + tq, :] = vc_ref[...]
    nk = prev + cq
    group = SWA_HEADS // SWA_KV_HEADS
    lane = lax.broadcasted_iota(jnp.int32, (1, LANES), 1)
    low = lane < SWA_HD
    rowi = lax.broadcasted_iota(jnp.int32, (2 * cq, 1), 0)
    top = rowi < cq
    coli = lax.broadcasted_iota(jnp.int32, (1, nk), 1)
    zero = jnp.zeros((nk, LANES), BF16)

    def scores(ci):
        r0 = ci * cq
        units = []
        for j in range(SWA_KV_HEADS):
            c0 = j * group * SWA_HD
            qst = (jnp.concatenate([q_ref[r0:r0 + cq, c0:c0 + LANES],
                                    q_ref[r0:r0 + cq, c0 + LANES:c0 + 2 * LANES]], axis=0).astype(F32)
                   * SWA_SCALE).astype(BF16)
            kk = kbuf[r0:r0 + nk, j * LANES:(j + 1) * LANES].astype(BF16)
            for half in range(2):
                sel = low if half == 0 else jnp.logical_not(low)
                units.append((j, half, lax.dot_general(qst, jnp.where(sel, kk, zero), (((1,), (1,)), ((), ())),
                                                       preferred_element_type=F32)))
        return ci, units

    def finish(ci, units):
        r0 = ci * cq
        masked = mask_history and ci < WIN_CHUNKS
        if masked:
            n_missing = jnp.maximum(WIN_CHUNKS - (i * (tq // cq) + ci), 0) * CHUNK
            valid = coli >= n_missing
        probs = []
        for j, half, s in units:
            if masked:
                s = jnp.where(valid, s, -jnp.inf)
            sink = jnp.where(top, sink_ref[j * group + half], sink_ref[j * group + 2 + half])
            m = jnp.maximum(jnp.max(s, axis=-1, keepdims=True), sink)
            p = jnp.exp(s - m)
            den = jnp.sum(p, axis=-1, keepdims=True) + jnp.exp(sink - m)
            probs.append((p / den).astype(BF16))
        for j in range(SWA_KV_HEADS):
            c0 = j * group * SWA_HD
            vv = vbuf[r0:r0 + nk, j * LANES:(j + 1) * LANES].astype(BF16)
            acc = (jnp.dot(probs[2 * j], jnp.where(low, vv, zero), preferred_element_type=F32)
                   + jnp.dot(probs[2 * j + 1], jnp.where(low, zero, vv), preferred_element_type=F32))
            o_ref[r0:r0 + cq, c0:c0 + LANES] = acc[:cq].astype(o_ref.dtype)
            o_ref[r0:r0 + cq, c0 + LANES:c0 + 2 * LANES] = acc[cq:].astype(o_ref.dtype)

    pending = scores(0)
    for ci in range(1, tq // cq):
        nxt = scores(ci)
        finish(*pending)
        pending = nxt
    finish(*pending)


def _swa_call(sinks, q, k_prev, k_cur, v_prev, v_cur, prev_spec, batch, t, tq, cq, prev, mask_history):
    nt = t // tq
    tok = lambda b, i: (b * nt + i, 0)
    return pl.pallas_call(
        functools.partial(_swa_kernel, tq=tq, cq=cq, prev=prev, mask_history=mask_history),
        grid=(batch, nt),
        in_specs=[
            pl.BlockSpec(memory_space=pltpu.SMEM),
            pl.BlockSpec((tq, SWA_Q), tok),
            prev_spec,
            pl.BlockSpec((tq, SWA_KV_DUP), tok),
            prev_spec,
            pl.BlockSpec((tq, SWA_KV_DUP), tok),
        ],
        out_specs=pl.BlockSpec((tq, SWA_Q), tok),
        out_shape=jax.ShapeDtypeStruct((batch * t, SWA_Q), q.dtype),
        scratch_shapes=[pltpu.VMEM((prev + tq, SWA_KV_DUP), k_cur.dtype),
                        pltpu.VMEM((prev + tq, SWA_KV_DUP), v_cur.dtype)],
        compiler_params=_params(2),
        name="sliding_window_attention",
    )(sinks.astype(F32), q, k_prev, k_cur, v_prev, v_cur)


def _swa_prompt(sinks, q, k, v, batch, t):
    tq = min(TOKEN_TILE, t)
    nt = t // tq
    per = tq // WINDOW
    prev_spec = pl.BlockSpec((WINDOW, SWA_KV_DUP), lambda b, i: (jnp.maximum((b * nt + i) * per - 1, 0), 0))
    return _swa_call(sinks, q, k, k, v, v, prev_spec, batch, t, tq, CHUNK, WINDOW, True)


def _swa_sample(sinks, q, k, v, cache_k, cache_v, batch, t):
    n_keep = cache_k.shape[1]
    q_pos = PAST_LEN + np.arange(t)
    k_pos = np.concatenate([PAST_LEN - n_keep + np.arange(n_keep), q_pos])
    qc = (q_pos // CHUNK)[:, None]
    kc = (k_pos // CHUNK)[None, :]
    assert np.all((kc <= qc) & (kc >= qc - WIN_CHUNKS)), "sample step expects every cached row in window"
    prev_spec = pl.BlockSpec((n_keep, SWA_KV_DUP), lambda b, i: (b, 0))
    return _swa_call(sinks, q, cache_k.reshape(batch * n_keep, SWA_KV_DUP), k,
                     cache_v.reshape(batch * n_keep, SWA_KV_DUP), v, prev_spec, batch, t, t, t, n_keep, False)


def _dup_heads(t):
    return jnp.concatenate([t[..., 0, :], t[..., 0, :], t[..., 1, :], t[..., 1, :]], axis=-1)


def _undup_heads(t):
    return jnp.stack([t[:, :SWA_HD], t[:, 2 * SWA_HD:3 * SWA_HD]], axis=1).astype(F32)


def _memattn_kernel(q_ref, mk_ref, mv_ref, o_ref, *, rows):
    tt = q_ref.shape[0]

    def scores(h, r0):
        ls = slice(h * XA_HD, (h + 1) * XA_HD)
        s = lax.dot_general(q_ref[r0:r0 + rows, ls].astype(BF16), mk_ref[:, ls].astype(BF16),
                            (((1,), (1,)), ((), ())), preferred_element_type=F32)
        return h, r0, s

    def finish(h, r0, s):
        ls = slice(h * XA_HD, (h + 1) * XA_HD)
        s = s * (XA_HD ** -0.5)
        m = jnp.max(s, axis=-1, keepdims=True)
        p = jnp.exp(s - m)
        p = (p / jnp.sum(p, axis=-1, keepdims=True)).astype(BF16)
        o_ref[r0:r0 + rows, ls] = jnp.dot(p, mv_ref[:, ls].astype(BF16),
                                          preferred_element_type=F32).astype(o_ref.dtype)

    blocks = [(h, r0) for h in range(XA_HEADS) for r0 in range(0, tt, rows)]
    pending = scores(*blocks[0])
    for blk in blocks[1:]:
        nxt = scores(*blk)
        finish(*pending)
        pending = nxt
    finish(*pending)


def _mem_attention(q, mk, mv, batch, t):
    tt = min(TOKEN_TILE, t)
    nt = t // tt
    tok = lambda b, i: (b * nt + i, 0)
    mem = pl.BlockSpec((None, N_MEM, XA_Q), lambda b, i: (b, 0, 0))
    return pl.pallas_call(
        functools.partial(_memattn_kernel, rows=min(256, tt)),
        grid=(batch, nt),
        in_specs=[pl.BlockSpec((tt, XA_Q), tok), mem, mem],
        out_specs=pl.BlockSpec((tt, XA_Q), tok),
        out_shape=jax.ShapeDtypeStruct((batch * t, XA_Q), q.dtype),
        compiler_params=_params(2),
        name="memory_attention",
    )(q, mk, mv)


MERGE_ROWS = 256


def _merge_kernel(x_ref, og_ref, os_ref, ox_ref, gate_ref, wb_ref, wo_ref, g_ref, b_ref, out_ref):
    rows = min(MERGE_ROWS, x_ref.shape[0])

    def branches(r0):
        rs = slice(r0, r0 + rows)
        return r0, [jnp.dot(ref[rs, :].astype(BF16), wb_ref[bi], preferred_element_type=F32)
                    for bi, ref in enumerate((og_ref, os_ref, ox_ref))]

    def finish(r0, ys):
        rs = slice(r0, r0 + rows)
        merged = None
        for bi, y in enumerate(ys):
            y = gate_ref[rs, bi * D_MODEL:(bi + 1) * D_MODEL].astype(F32) * y
            merged = y if merged is None else merged + y
        u = jnp.dot(merged.astype(BF16), wo_ref[...], preferred_element_type=F32)
        out_ref[rs, :] = _layer_norm(DN_ALPHA * x_ref[rs, :] + u, g_ref[...], b_ref[...])

    pending = branches(0)
    for r0 in range(rows, x_ref.shape[0], rows):
        nxt = branches(r0)
        finish(*pending)
        pending = nxt
    finish(*pending)


def _merge(x2d, o_g, o_s, o_x, gate, w_branch, w_o, layer, ln_g, ln_b):
    m = x2d.shape[0]
    tm = min(TOKEN_TILE, m)
    row = lambda w: pl.BlockSpec((tm, w), lambda i: (i, 0))
    return pl.pallas_call(
        _merge_kernel,
        grid=(m // tm,),
        in_specs=[row(D_MODEL), row(BRANCH_W), row(BRANCH_W), row(BRANCH_W), row(N_BRANCH * D_MODEL),
                  _layer_spec((N_BRANCH, BRANCH_W, D_MODEL), layer), _layer_spec((D_MODEL, D_MODEL), layer),
                  _const_spec((1, D_MODEL)), _const_spec((1, D_MODEL))],
        out_specs=row(D_MODEL),
        out_shape=jax.ShapeDtypeStruct((m, D_MODEL), F32),
        compiler_params=_params(1),
        name="branch_merge",
    )(x2d, o_g, o_s, o_x, gate, w_branch, w_o, ln_g.reshape(1, D_MODEL), ln_b.reshape(1, D_MODEL))


FF_CHUNK = 1024


def _ffn_kernel(x_ref, wu_ref, wd_ref, g_ref, b_ref, out_ref):
    x = x_ref[...]
    xb = x.astype(BF16)
    def up(c0):
        return c0, jnp.dot(xb, wu_ref[:, c0:c0 + FF_CHUNK], preferred_element_type=F32)

    def down(c0, hid):
        hid = jnp.square(jnp.maximum(hid, 0.0)).astype(BF16)
        return jnp.dot(hid, wd_ref[c0:c0 + FF_CHUNK, :], preferred_element_type=F32)

    pending = up(0)
    acc = None
    for c0 in range(FF_CHUNK, D_FF, FF_CHUNK):
        nxt = up(c0)
        part = down(*pending)
        acc = part if acc is None else acc + part
        pending = nxt
    part = down(*pending)
    acc = part if acc is None else acc + part
    out_ref[...] = _layer_norm(DN_ALPHA * x + acc, g_ref[...], b_ref[...])


def _ffn(x2d, w_up, w_down, layer, ln_g, ln_b):
    m = x2d.shape[0]
    tm = min(TOKEN_TILE, m)
    row = pl.BlockSpec((tm, D_MODEL), lambda i: (i, 0))
    return pl.pallas_call(
        _ffn_kernel,
        grid=(m // tm,),
        in_specs=[row, _layer_spec((D_MODEL, D_FF), layer), _layer_spec((D_FF, D_MODEL), layer),
                  _const_spec((1, D_MODEL)), _const_spec((1, D_MODEL))],
        out_specs=row,
        out_shape=jax.ShapeDtypeStruct((m, D_MODEL), F32),
        compiler_params=_params(1),
        name="channel_mixer",
    )(x2d, w_up, w_down, ln_g.reshape(1, D_MODEL), ln_b.reshape(1, D_MODEL))


def _layer(x2d, batch, t, act_dtype, lw, conv0, s0, mk, mv, swa_cache):
    if conv0 is None:
        h = _in_projection(x2d, lw["w_in"], lw["layer"], act_dtype, lw["conv_w"], t)
        tiles = t // min(TOKEN_TILE, t)
        conv_new = h["conv_tail"].reshape(batch, tiles, CONV_HIST, GDN_CONV_CH)[:, -1, CONV_HIST - (CONV_W - 1):]
    else:
        h = _in_projection(x2d, lw["w_in"], lw["layer"], act_dtype)
    o_g, s_new, conv_out = _gdn(h["qkv"], h["z"], h["ab"], conv0, s0, lw["conv_w"], lw["a_log"], lw["dt_bias"],
                                lw["gdn_norm_w"], batch, t)
    if conv0 is not None:
        conv_new = conv_out
    if swa_cache is None:
        o_s = _swa_prompt(lw["attn_sinks"], h["qs"], h["ks"], h["vs"], batch, t)
    else:
        o_s = _swa_sample(lw["attn_sinks"], h["qs"], h["ks"], h["vs"], swa_cache[0], swa_cache[1], batch, t)
    o_x = _mem_attention(h["qx"], mk, mv, batch, t)
    x1 = _merge(x2d, o_g, o_s, o_x, h["gate"], lw["w_branch"], lw["w_o"], lw["layer"], lw["ln1_g"], lw["ln1_b"])
    x2 = _ffn(x1, lw["w_up"], lw["w_down"], lw["layer"], lw["ln2_g"], lw["ln2_b"])
    return x2, s_new, conv_new, h["ks"], h["vs"]


def kernel(x_prompt, x_sample, state_gdn_s, state_gdn_conv, cache_swa_k, cache_swa_v, cache_mem_k, cache_mem_v,
           mem_prompt, w_in, conv_w, a_log, dt_bias, gdn_norm_w, attn_sinks, w_mem_kv, w_branch, w_o,
           ln1_g, ln1_b, w_up, w_down, ln2_g, ln2_b):
    depth = w_in.shape[0]
    stacked = dict(w_in=_prep_w_in(w_in), w_mem_kv=w_mem_kv.astype(BF16), w_branch=w_branch.astype(BF16),
                   w_o=w_o.astype(BF16), w_up=w_up.astype(BF16), w_down=w_down.astype(BF16))
    layers = []
    for l in range(depth):
        layers.append(dict(
            stacked, layer=l, conv_w=conv_w[l], a_log=a_log[l], dt_bias=dt_bias[l],
            gdn_norm_w=gdn_norm_w[l], attn_sinks=attn_sinks[l], ln1_g=ln1_g[l], ln1_b=ln1_b[l],
            ln2_g=ln2_g[l], ln2_b=ln2_b[l]))

    bp, tp, _ = x_prompt.shape
    n_keep = cache_swa_k.shape[2]
    x = x_prompt.reshape(bp * tp, D_MODEL)
    mem2d = mem_prompt.reshape(bp * N_MEM, D_MODEL)
    p_s, p_conv, p_k, p_v, p_mk, p_mv = [], [], [], [], [], []
    for lw in layers:
        mkv = _matmul(mem2d, lw["w_mem_kv"], lw["layer"]).reshape(bp, N_MEM, 2 * XA_Q)
        mk, mv = mkv[..., :XA_Q], mkv[..., XA_Q:]
        s0 = jnp.zeros((bp, GDN_HEADS, GDN_DK, GDN_DV), F32)
        x, s_new, conv_new, ks, vs = _layer(x, bp, tp, BF16, lw, None, s0, mk, mv, None)
        p_s.append(s_new)
        p_conv.append(conv_new)
        p_k.append(_undup_heads(ks.reshape(bp, tp, SWA_KV_DUP)[:, tp - n_keep:].reshape(bp * n_keep, SWA_KV_DUP))
                   .reshape(bp, n_keep, SWA_KV_HEADS, SWA_HD))
        p_v.append(_undup_heads(vs.reshape(bp, tp, SWA_KV_DUP)[:, tp - n_keep:].reshape(bp * n_keep, SWA_KV_DUP))
                   .reshape(bp, n_keep, SWA_KV_HEADS, SWA_HD))
        p_mk.append(mk.reshape(bp, N_MEM, XA_HEADS, XA_HD))
        p_mv.append(mv.reshape(bp, N_MEM, XA_HEADS, XA_HD))
    y_prompt = x.reshape(bp, tp, D_MODEL)

    bs, ts, _ = x_sample.shape
    x = x_sample.reshape(bs * ts, D_MODEL)
    s_s, s_conv, s_k, s_v = [], [], [], []
    for l, lw in enumerate(layers):
        cache = (_dup_heads(cache_swa_k[l]), _dup_heads(cache_swa_v[l]))
        x, s_new, conv_new, ks, vs = _layer(
            x, bs, ts, F32, lw, state_gdn_conv[l], state_gdn_s[l],
            cache_mem_k[l].reshape(bs, N_MEM, XA_Q), cache_mem_v[l].reshape(bs, N_MEM, XA_Q), cache)
        s_s.append(s_new)
        s_conv.append(conv_new)
        k_new = _undup_heads(ks).reshape(bs, ts, SWA_KV_HEADS, SWA_HD)
        v_new = _undup_heads(vs).reshape(bs, ts, SWA_KV_HEADS, SWA_HD)
        s_k.append(jnp.concatenate([cache_swa_k[l], k_new], axis=1)[:, -n_keep:])
        s_v.append(jnp.concatenate([cache_swa_v[l], v_new], axis=1)[:, -n_keep:])
    y_sample = x.reshape(bs, ts, D_MODEL)

    return (y_prompt, y_sample,
            jnp.stack(p_s), jnp.stack(p_conv), jnp.stack(p_k), jnp.stack(p_v), jnp.stack(p_mk), jnp.stack(p_mv),
            jnp.stack(s_s), jnp.stack(s_conv), jnp.stack(s_k), jnp.stack(s_v))
```

```python
import functools

import jax
import jax.numpy as jnp
import numpy as np
from jax import lax
from jax.experimental import pallas as pl
from jax.experimental.pallas import tpu as pltpu

F32 = jnp.float32
BF16 = jnp.bfloat16
HI = lax.Precision.HIGHEST

D_MODEL = 1024
DEPTH = 2
CHUNK = 64
PAST_LEN = 2048
GDN_HEADS = 4
GDN_DK = 128
GDN_DV = 128
CONV_W = 4
SWA_HEADS = 8
SWA_KV_HEADS = 2
SWA_HD = 64
WINDOW = 128
WIN_CHUNKS = WINDOW // CHUNK
SWA_SCALE = SWA_HD ** -0.5
assert np.log2(SWA_SCALE) == round(np.log2(SWA_SCALE)), "the scale is folded into q before a bf16 cast"
N_MEM = 256
XA_HEADS = 4
XA_HD = 128
D_FF = 4 * D_MODEL
N_BRANCH = 3
BRANCH_W = 512
DN_ALPHA = (2.0 * DEPTH) ** 0.25
LN_EPS = 1e-5
NORM_EPS = 1e-6
GDN_QK = GDN_HEADS * GDN_DK
GDN_V = GDN_HEADS * GDN_DV
GDN_CONV_CH = 2 * GDN_QK + GDN_V
SWA_Q = SWA_HEADS * SWA_HD
SWA_KV = SWA_KV_HEADS * SWA_HD
XA_Q = XA_HEADS * XA_HD
IN_SPLITS = (GDN_CONV_CH, GDN_HEADS, GDN_HEADS, GDN_V, SWA_Q, SWA_KV, SWA_KV, XA_Q, N_BRANCH * D_MODEL)

LANES = 128
SUBLANES = 8
TOKEN_TILE = 512
VMEM_LIMIT = 56 * 2**20

SWA_KV_DUP = 2 * SWA_KV
AB_PAD = LANES
PROJ_OUT = (("qkv", GDN_CONV_CH), ("z", GDN_V), ("qs", SWA_Q), ("qx", XA_Q), ("ks", SWA_KV_DUP),
            ("vs", SWA_KV_DUP), ("ab", AB_PAD), ("gate", N_BRANCH * D_MODEL))
PROJ_COLS = sum(w for _, w in PROJ_OUT)
PROJ_CHUNK = 512
CONV_CHUNK = 256


def _params(n_grid):
    return pltpu.CompilerParams(dimension_semantics=("arbitrary",) * n_grid, vmem_limit_bytes=VMEM_LIMIT)


def _const_spec(shape):
    zeros = (0,) * len(shape)
    return pl.BlockSpec(shape, lambda *_: zeros, pipeline_mode=pl.Buffered(1))


def _layer_spec(shape, layer):
    zeros = (0,) * len(shape)
    return pl.BlockSpec((None,) + tuple(shape), lambda *_: (layer,) + zeros, pipeline_mode=pl.Buffered(1))


def _sigmoid(x):
    return 1.0 / (1.0 + jnp.exp(-x))


def _softplus(x):
    return jnp.maximum(x, 0.0) + jnp.log1p(jnp.exp(-jnp.abs(x)))


def _layer_norm(r, g, b):
    mu = jnp.mean(r, axis=-1, keepdims=True)
    d = r - mu
    var = jnp.mean(d * d, axis=-1, keepdims=True)
    return d * lax.rsqrt(var + LN_EPS) * g + b


CONV_HIST = SUBLANES
CONV_ROWS = 128


def _zero_after(y, r0=0):
    bits = pltpu.bitcast(y[r0:r0 + SUBLANES, 0:LANES], jnp.uint32)
    return ((bits >> 16) >> 16)[0:1, :].astype(F32)


def _conv_silu_norm(cw_ref, buf, r0, rows, cg, buf_lane0=0, anchor=None):
    ls = slice(cg * LANES, (cg + 1) * LANES)
    bl = slice(cg * LANES - buf_lane0, (cg + 1) * LANES - buf_lane0)
    acc = None
    for w in range(CONV_W):
        s = CONV_HIST - (CONV_W - 1) + w + r0
        tap = cw_ref[w:w + 1, ls]
        if w == 0 and anchor is not None:
            tap = tap + anchor
        term = tap * buf[s:s + rows, bl]
        acc = term if acc is None else acc + term
    y = acc * _sigmoid(acc)
    if cg < 2 * GDN_HEADS:
        inv = lax.rsqrt(jnp.sum(y * y, axis=-1, keepdims=True) + NORM_EPS)
        if cg < GDN_HEADS:
            inv = inv * (GDN_DK ** -0.5)
        y = y * inv
    return y


def _inproj_kernel(*refs, tiles_per_stream):
    fuse_conv = tiles_per_stream is not None
    n_out = len(PROJ_OUT)
    x_ref, w_refs, refs = refs[0], refs[1:1 + n_out], refs[1 + n_out:]
    if fuse_conv:
        cw_ref = refs[0]
        out_refs, tail_ref, ybufs, xb_s = refs[1:1 + n_out], refs[1 + n_out], refs[2 + n_out:-1], refs[-1]

        @pl.when(pl.program_id(0) % tiles_per_stream == 0)
        def _():
            for ybuf in ybufs:
                ybuf[0:CONV_HIST, :] = jnp.zeros((CONV_HIST, CONV_CHUNK), F32)
    else:
        out_refs, xb_s = refs[:-1], refs[-1]
    tm = x_ref.shape[0]
    xb_s[...] = x_ref[...].astype(BF16)
    chunks = []
    for (name, width), ref, w_ref in zip(PROJ_OUT, out_refs, w_refs):
        step = CONV_CHUNK if (fuse_conv and name == "qkv") else PROJ_CHUNK
        chunks += [(name, ref, w_ref, c0, min(step, width - c0)) for c0 in range(0, width, step)]
    if fuse_conv:
        pre = [ch for ch in chunks if ch[0] == "qkv"]
        rest = [ch for ch in chunks if ch[0] != "qkv"]
        per = -(-len(rest) // len(pre))
        chunks = []
        for n, ch in enumerate(pre):
            chunks += [ch] + rest[n * per:(n + 1) * per]
    pending = []
    for n, (name, ref, w_ref, c0, cw) in enumerate(chunks):
        y = jnp.dot(xb_s[...], w_ref[:, c0:c0 + cw], preferred_element_type=F32)
        if name == "qkv" and fuse_conv:
            ybuf = ybufs[c0 // CONV_CHUNK]
            ybuf[CONV_HIST:CONV_HIST + tm, :] = y
            tail_ref[:, c0:c0 + cw] = ybuf[tm:tm + CONV_HIST, :]
            pending = [(ybuf, c0, cg, r0) for cg in range(c0 // LANES, (c0 + cw) // LANES)
                       for r0 in range(0, tm, CONV_ROWS)]
            n_follow = len([ch for ch in chunks[n + 1:n + 1 + per] if ch[0] != "qkv"])
            share = -(-len(pending) // max(n_follow, 1))
            continue
        if name == "gate":
            y = _sigmoid(y)
        ref[:, c0:c0 + cw] = y.astype(ref.dtype)
        for k, (ybuf, lane0, cg, r0) in enumerate(pending[:share] if fuse_conv else []):
            anchor = _zero_after(y, (k * tm // share) // SUBLANES * SUBLANES)
            out_refs[0][r0:r0 + CONV_ROWS, cg * LANES:(cg + 1) * LANES] = _conv_silu_norm(
                cw_ref, ybuf, r0, CONV_ROWS, cg, lane0, anchor).astype(out_refs[0].dtype)
        pending = pending[share:] if fuse_conv else pending
    assert not pending
    if fuse_conv:
        for ybuf in ybufs:
            ybuf[0:CONV_HIST, :] = ybuf[tm:tm + CONV_HIST, :]


def _in_projection(x2d, w_all, layer, act_dtype, conv_w=None, stream_len=None):
    m = x2d.shape[0]
    tm = min(TOKEN_TILE, m)
    fuse_conv = conv_w is not None
    dtypes = {"ab": F32}
    out_shape = [jax.ShapeDtypeStruct((m, w), dtypes.get(n, act_dtype)) for n, w in PROJ_OUT]
    out_specs = [pl.BlockSpec((tm, w), lambda i: (i, 0)) for _, w in PROJ_OUT]
    in_specs = [pl.BlockSpec((tm, D_MODEL), lambda i: (i, 0))] + [_layer_spec((D_MODEL, w), layer) for _, w in PROJ_OUT]
    args = [x2d] + [w_all[n] for n, _ in PROJ_OUT]
    names = [n for n, _ in PROJ_OUT]
    scratch = []
    if fuse_conv:
        assert stream_len % tm == 0 and tm % CONV_ROWS == 0
        in_specs.append(_const_spec((CONV_W, GDN_CONV_CH)))
        args.append(conv_w.astype(F32))
        out_shape.append(jax.ShapeDtypeStruct((m // tm * CONV_HIST, GDN_CONV_CH), F32))
        out_specs.append(pl.BlockSpec((CONV_HIST, GDN_CONV_CH), lambda i: (i, 0)))
        names.append("conv_tail")
        assert GDN_CONV_CH % CONV_CHUNK == 0
        scratch += [pltpu.VMEM((CONV_HIST + tm, CONV_CHUNK), F32) for _ in range(GDN_CONV_CH // CONV_CHUNK)]
    scratch.append(pltpu.VMEM((tm, D_MODEL), BF16))
    outs = pl.pallas_call(
        functools.partial(_inproj_kernel, tiles_per_stream=stream_len // tm if fuse_conv else None),
        grid=(m // tm,),
        in_specs=in_specs,
        out_specs=out_specs,
        out_shape=out_shape,
        scratch_shapes=scratch,
        compiler_params=_params(1),
        name="in_projection",
    )(*args)
    return dict(zip(names, outs))


def _prep_w_in(w):
    offs = np.cumsum((0,) + IN_SPLITS)
    qkv, a, b, z, qs, ks, vs, qx, gate = [w[..., offs[i]:offs[i + 1]] for i in range(len(IN_SPLITS))]

    def dup(t):
        return [t[..., :SWA_HD], t[..., :SWA_HD], t[..., SWA_HD:], t[..., SWA_HD:]]

    pad = jnp.zeros(w.shape[:-1] + (AB_PAD - 2 * GDN_HEADS,), w.dtype)
    groups = dict(qkv=qkv, z=z, qs=qs, qx=qx, ks=jnp.concatenate(dup(ks), axis=-1),
                  vs=jnp.concatenate(dup(vs), axis=-1), ab=jnp.concatenate([a, b, pad], axis=-1), gate=gate)
    assert all(groups[n].shape[-1] == width for n, width in PROJ_OUT)
    return {n: g.astype(BF16) for n, g in groups.items()}


def _mem_kv_kernel(x_ref, w_ref, mk_ref, mv_ref, mkb_ref, mvb_ref):
    y = jnp.dot(x_ref[...].astype(BF16), w_ref[...], preferred_element_type=F32)
    mkb_ref[...] = y[:, :XA_Q].astype(BF16)
    mvb_ref[...] = y[:, XA_Q:].astype(BF16)
    for h in range(XA_HEADS):
        mk_ref[:, h, :] = y[:, h * XA_HD:(h + 1) * XA_HD]
        mv_ref[:, h, :] = y[:, XA_Q + h * XA_HD:XA_Q + (h + 1) * XA_HD]


def _mem_kv(mem2d, w_stacked):
    m, k = mem2d.shape
    depth = w_stacked.shape[0]
    tm = min(TOKEN_TILE, m)
    out = pl.BlockSpec((None, tm, XA_HEADS, XA_HD), lambda l, i: (l, i, 0, 0))
    shape = jax.ShapeDtypeStruct((depth, m, XA_HEADS, XA_HD), F32)
    out_b = pl.BlockSpec((None, tm, XA_Q), lambda l, i: (l, i, 0))
    shape_b = jax.ShapeDtypeStruct((depth, m, XA_Q), BF16)
    return pl.pallas_call(
        _mem_kv_kernel,
        grid=(depth, m // tm),
        in_specs=[pl.BlockSpec((tm, k), lambda l, i: (i, 0)),
                  pl.BlockSpec((None, k, 2 * XA_Q), lambda l, i: (l, 0, 0))],
        out_specs=[out, out, out_b, out_b],
        out_shape=[shape, shape, shape_b, shape_b],
        compiler_params=_params(2),
        name="mem_kv_projection",
    )(mem2d, w_stacked)


def _gdn_kernel(*refs, tt, c, ua, conv_done):
    refs = list(refs)
    if conv_done:
        (qkv_ref, z_ref, ab_ref, s0_ref, alog_ref, dtb_ref, nw_ref, o_ref, sout_ref,
         gbuf, bbuf, s_scr, u_s, w_s, qe_s, kdt_s, qk_s, el_s) = refs
    else:
        (qkv_ref, z_ref, ab_ref, conv0_ref, s0_ref, cw_ref, alog_ref, dtb_ref, nw_ref, o_ref, sout_ref, convout_ref,
         xbuf, cbuf, gbuf, bbuf, s_scr, u_s, w_s, qe_s, kdt_s, qk_s, el_s) = refs
    i = pl.program_id(1)

    @pl.when(i == 0)
    def _():
        s_scr[...] = s0_ref[...]

    if not conv_done:
        @pl.when(i == 0)
        def _():
            xbuf[0:CONV_HIST, :] = conv0_ref[...]

        @pl.when(i > 0)
        def _():
            xbuf[0:CONV_HIST, :] = xbuf[tt:tt + CONV_HIST, :]

        xbuf[CONV_HIST:CONV_HIST + tt, :] = qkv_ref[...].astype(F32)

    def conv_out(rows, cols):
        return qkv_ref[rows, cols].astype(F32) if conv_done else cbuf[rows, cols]

    expalog = jnp.exp(alog_ref[...])
    dtb = dtb_ref[...]

    def prepass(ci):
        r0 = ci * c
        if not conv_done:
            for cg in range(GDN_CONV_CH // LANES):
                cbuf[r0:r0 + c, cg * LANES:(cg + 1) * LANES] = _conv_silu_norm(cw_ref, xbuf, r0, c, cg)
        ab = ab_ref[r0:r0 + c, :]
        gbuf[r0:r0 + c, :] = -expalog * _softplus(ab + dtb)
        bbuf[r0:r0 + c, :] = _sigmoid(ab)

    pairs = GDN_HEADS // 2
    row = lax.broadcasted_iota(jnp.int32, (c, 2 * c), 0)
    lane = lax.broadcasted_iota(jnp.int32, (c, 2 * c), 1)
    left = lane < c
    col = jnp.where(left, lane, lane - c)
    causal = row >= col
    strict = row > col
    l_strict = strict.astype(F32)
    l_incl2 = causal.astype(BF16)
    n_square = max(int(np.ceil(np.log2(c))) - 1, 0)
    nw = nw_ref[...]
    nt_dims = (((1,), (1,)), ((), ()))

    def split2(x):
        hi = x.astype(BF16)
        lo = (x - hi.astype(F32)).astype(BF16)
        return hi, lo

    def block_diag(a, b):
        zero = jnp.zeros_like(a)
        return jnp.concatenate([jnp.concatenate([a, zero], axis=1), jnp.concatenate([zero, b], axis=1)], axis=0)

    def pair_diag(p):
        zero = jnp.zeros_like(p)
        return jnp.concatenate([jnp.where(left, p, zero), jnp.where(left, zero, p)], axis=0)

    def phase_a(chunk_ids):
        chunks = []
        for ci in chunk_ids:
            rows = slice(ci * c, (ci + 1) * c)
            g_hi, g_lo = split2(gbuf[rows, :])
            gc_all = jnp.dot(l_incl2, jnp.concatenate([g_hi, g_lo], axis=0), preferred_element_type=F32)
            chunks.append((ci, rows, g_hi.astype(F32), g_lo.astype(F32), gc_all))
        yield
        chains = []
        for ci, rows, g_hi, g_lo, gc_all in chunks:
            el_s[ci * SUBLANES:(ci + 1) * SUBLANES, :] = jnp.broadcast_to(jnp.exp(gc_all[c - 1:c, :]),
                                                                         (SUBLANES, LANES))
            b_all = bbuf[rows, :]
            for hp in range(pairs):
                ha, hb = 2 * hp, 2 * hp + 1
                beta = [b_all[:, GDN_HEADS + h:GDN_HEADS + h + 1] for h in (ha, hb)]
                steps = jnp.concatenate(
                    [(jnp.where(left, g[:, ha:ha + 1], g[:, hb:hb + 1]) * l_strict).astype(BF16)
                     for g in (g_hi, g_lo)], axis=0)
                diff = jnp.dot(l_incl2, steps, preferred_element_type=F32)
                q2 = conv_out(rows, slice(ha * GDN_DK, (hb + 1) * GDN_DK))
                k = [conv_out(rows, slice(GDN_QK + h * GDN_DK, GDN_QK + (h + 1) * GDN_DK)) for h in (ha, hb)]
                kb2 = jnp.concatenate([k[0] * beta[0], k[1] * beta[1]], axis=1)
                gram = lax.dot_general(jnp.concatenate([kb2.astype(BF16), q2.astype(BF16)], axis=0),
                                       block_diag(k[0].astype(BF16), k[1].astype(BF16)), nt_dims,
                                       preferred_element_type=F32)
                chains.append(dict(ci=ci, rows=rows, hp=hp, heads=(ha, hb), beta=beta, diff=diff, gram=gram,
                                   gc=[gc_all[:, h:h + 1] for h in (ha, hb)],
                                   g_last=[gc_all[c - 1:c, h:h + 1] for h in (ha, hb)]))
        yield
        for ch in chains:
            decay = jnp.where(causal, jnp.exp(ch.pop("diff")), 0.0)
            gram = ch.pop("gram")
            qk_s[ch["ci"] * pairs + ch["hp"]] = (gram[c:] * decay).astype(BF16)
            ch["toff"] = -jnp.where(strict, gram[:c] * decay, 0.0)
            ch["pw"] = ch["toff"].astype(BF16)
        for _ in range(n_square):
            for ch in chains:
                ch["pw"] = jnp.dot(ch["pw"], pair_diag(ch["pw"]), preferred_element_type=F32)
            yield
            for ch in chains:
                pwb = ch["pw"].astype(BF16)
                ch["toff"] = ch["toff"] + ch["pw"] + jnp.dot(ch["toff"].astype(BF16), pair_diag(pwb),
                                                             preferred_element_type=F32)
                ch["pw"] = pwb
            yield
        for ch in chains:
            rows = ch["rows"]
            rhs, kdec = [], []
            for n, h in enumerate(ch["heads"]):
                egc = jnp.exp(ch["gc"][n])
                k = conv_out(rows, slice(GDN_QK + h * GDN_DK, GDN_QK + (h + 1) * GDN_DK))
                v = conv_out(rows, slice(2 * GDN_QK + h * GDN_DV, 2 * GDN_QK + (h + 1) * GDN_DV))
                rhs.append(jnp.concatenate([v * ch["beta"][n], k * (ch["beta"][n] * egc)], axis=1))
                kdec.append(k * jnp.exp(ch["g_last"][n] - ch["gc"][n]))
                qe_s[rows, h * GDN_DK:(h + 1) * GDN_DK] = (conv_out(rows, slice(h * GDN_DK, (h + 1) * GDN_DK))
                                                           * egc).astype(BF16)
            uw = jnp.dot(ch["toff"].astype(BF16), block_diag(rhs[0].astype(BF16), rhs[1].astype(BF16)),
                         preferred_element_type=F32)
            for n, h in enumerate(ch["heads"]):
                hq = slice(h * GDN_DK, (h + 1) * GDN_DK)
                uw_h = rhs[n] + uw[:, n * (GDN_DV + GDN_DK):(n + 1) * (GDN_DV + GDN_DK)]
                u_s[rows, hq] = uw_h[:, :GDN_DV]
                w_s[rows, hq] = uw_h[:, GDN_DV:].astype(BF16)
            kdt_s[ch["ci"] * pairs + ch["hp"]] = jnp.concatenate(kdec, axis=0).T.astype(BF16)
        yield

    heads = range(GDN_HEADS)
    hqs = [slice(h * GDN_DK, (h + 1) * GDN_DK) for h in heads]
    state = [s_scr[h] for h in heads]

    def phase_b(chunk_ids):
        for ci in chunk_ids:
            rows = slice(ci * c, (ci + 1) * c)
            e_last = el_s[ci * SUBLANES:ci * SUBLANES + 1, :]
            ws = [jnp.dot(jnp.concatenate([w_s[rows, hqs[h]], qe_s[rows, hqs[h]]], axis=0),
                          state[h].astype(BF16), preferred_element_type=F32) for h in heads]
            yield
            v_new = [(u_s[rows, hqs[h]] - ws[h][:c]).astype(BF16) for h in heads]
            v_diag = [block_diag(v_new[2 * hp], v_new[2 * hp + 1]) for hp in range(pairs)]
            s_inc = [jnp.dot(kdt_s[ci * pairs + hp], v_diag[hp], preferred_element_type=F32)
                     for hp in range(pairs)]
            o_intra = [jnp.dot(qk_s[ci * pairs + hp], v_diag[hp], preferred_element_type=F32)
                       for hp in range(pairs)]
            yield
            for h in heads:
                hp, side = divmod(h, 2)
                lanes = slice(side * GDN_DV, (side + 1) * GDN_DV)
                state[h] = state[h] * e_last[:, h:h + 1] + s_inc[hp][:, lanes]
                o = ws[h][c:] + o_intra[hp][:, lanes]
                o = o * lax.rsqrt(jnp.mean(o * o, axis=-1, keepdims=True) + NORM_EPS) * nw
                zz = z_ref[rows, hqs[h]].astype(F32)
                o_ref[rows, hqs[h]] = (o * (zz * _sigmoid(zz))).astype(o_ref.dtype)

    def alternate(*gens):
        gens = list(gens)
        while gens:
            for g in list(gens):
                if next(g, StopIteration) is StopIteration:
                    gens.remove(g)

    groups = [list(range(g0, g0 + ua)) for g0 in range(0, tt // c, ua)]
    def prepass_chunks(chunk_ids):
        for ci in chunk_ids:
            prepass(ci)
            yield

    alternate(prepass_chunks(groups[0]))
    for gi, grp in enumerate(groups):
        gens = [phase_a(grp)]
        if gi > 0:
            gens.append(phase_b(groups[gi - 1]))
        if gi + 1 < len(groups):
            gens.append(prepass_chunks(groups[gi + 1]))
        alternate(*gens)
    alternate(phase_b(groups[-1]))
    for h in heads:
        s_scr[h] = state[h]

    @pl.when(i == pl.num_programs(1) - 1)
    def _():
        sout_ref[...] = s_scr[...]
        if not conv_done:
            convout_ref[...] = xbuf[CONV_HIST + tt - (CONV_W - 1):CONV_HIST + tt, :]


GDN_PHASE_A_UNROLL = 4
GDN_TOKEN_TILE = 1024


def _gdn(qkv, z, ab, conv0, s0, conv_w, a_log, dt_bias, norm_w, batch, t):
    c = min(CHUNK, t)
    tt = min(GDN_TOKEN_TILE, t)
    nt = t // tt
    n_chunks = tt // c
    ua = min(GDN_PHASE_A_UNROLL, n_chunks)
    conv_done = conv0 is None

    def pad_vec(vv):
        return jnp.zeros((1, LANES), F32).at[0, :vv.shape[0]].set(vv.astype(F32))

    tok = lambda b, i: (b * nt + i, 0)
    state_spec = pl.BlockSpec((None, GDN_HEADS, GDN_DK, GDN_DV), lambda b, i: (b, 0, 0, 0))
    in_specs = [pl.BlockSpec((tt, GDN_CONV_CH), tok), pl.BlockSpec((tt, GDN_V), tok), pl.BlockSpec((tt, AB_PAD), tok)]
    args = [qkv, z, ab]
    if not conv_done:
        in_specs.append(pl.BlockSpec((None, CONV_HIST, GDN_CONV_CH), lambda b, i: (b, 0, 0)))
        args.append(jnp.concatenate([jnp.zeros((batch, CONV_HIST - (CONV_W - 1), GDN_CONV_CH), F32),
                                     conv0.astype(F32)], axis=1))
    in_specs.append(state_spec)
    args.append(s0.astype(F32))
    if not conv_done:
        in_specs.append(_const_spec((CONV_W, GDN_CONV_CH)))
        args.append(conv_w.astype(F32))
    in_specs += [_const_spec((1, LANES)), _const_spec((1, LANES)), _const_spec((1, GDN_DV))]
    args += [pad_vec(a_log), pad_vec(dt_bias), norm_w.astype(F32).reshape(1, GDN_DV)]
    out_specs = [pl.BlockSpec((tt, GDN_V), tok), state_spec]
    out_shape = [jax.ShapeDtypeStruct((batch * t, GDN_V), z.dtype),
                 jax.ShapeDtypeStruct((batch, GDN_HEADS, GDN_DK, GDN_DV), F32)]
    scratch = []
    if not conv_done:
        out_specs.append(pl.BlockSpec((None, CONV_W - 1, GDN_CONV_CH), lambda b, i: (b, 0, 0)))
        out_shape.append(jax.ShapeDtypeStruct((batch, CONV_W - 1, GDN_CONV_CH), F32))
        scratch += [pltpu.VMEM((tt + CONV_HIST, GDN_CONV_CH), F32),
                    pltpu.VMEM((tt, GDN_CONV_CH), F32)]
    scratch += [
        pltpu.VMEM((tt, LANES), F32),
        pltpu.VMEM((tt, LANES), F32),
        pltpu.VMEM((GDN_HEADS, GDN_DK, GDN_DV), F32),
        pltpu.VMEM((tt, GDN_V), F32),
        pltpu.VMEM((tt, GDN_QK), BF16),
        pltpu.VMEM((tt, GDN_QK), BF16),
        pltpu.VMEM((n_chunks * GDN_HEADS // 2, GDN_DK, 2 * c), BF16),
        pltpu.VMEM((n_chunks * GDN_HEADS // 2, c, 2 * c), BF16),
        pltpu.VMEM((n_chunks * SUBLANES, LANES), F32),
    ]
    outs = pl.pallas_call(
        functools.partial(_gdn_kernel, tt=tt, c=c, ua=ua, conv_done=conv_done),
        grid=(batch, nt),
        in_specs=in_specs,
        out_specs=out_specs,
        out_shape=out_shape,
        scratch_shapes=scratch,
        compiler_params=_params(2),
        name="gated_deltanet",
    )(*args)
    return outs[0], outs[1], (None if conv_done else outs[2])


def _swa_kernel(sink_ref, q_ref, kp_ref, kc_ref, vp_ref, vc_ref, o_ref, kbuf, vbuf, *, tq, cq, prev,
                mask_history):
    i = pl.program_id(1)
    kbuf[0:prev, :] = kp_ref[...]
    kbuf[prev:prev + tq, :] = kc_ref[...]
    vbuf[0:prev, :] = vp_ref[...]
    vbuf[prev:prev + tq, :] = vc_ref[...]
    nk = prev + cq
    group = SWA_HEADS // SWA_KV_HEADS
    lane = lax.broadcasted_iota(jnp.int32, (1, LANES), 1)
    low = lane < SWA_HD
    rowi = lax.broadcasted_iota(jnp.int32, (2 * cq, 1), 0)
    top = rowi < cq
    coli = lax.broadcasted_iota(jnp.int32, (1, nk), 1)
    zero = jnp.zeros((nk, LANES), BF16)

    def scores(ci):
        r0 = ci * cq
        units = []
        for j in range(SWA_KV_HEADS):
            c0 = j * group * SWA_HD
            qst = (jnp.concatenate([q_ref[r0:r0 + cq, c0:c0 + LANES],
                                    q_ref[r0:r0 + cq, c0 + LANES:c0 + 2 * LANES]], axis=0).astype(F32)
                   * SWA_SCALE).astype(BF16)
            kk = kbuf[r0:r0 + nk, j * LANES:(j + 1) * LANES].astype(BF16)
            for half in range(2):
                sel = low if half == 0 else jnp.logical_not(low)
                units.append((j, half, lax.dot_general(qst, jnp.where(sel, kk, zero), (((1,), (1,)), ((), ())),
                                                       preferred_element_type=F32)))
        return ci, units

    def finish(ci, units):
        r0 = ci * cq
        masked = mask_history and ci < WIN_CHUNKS
        if masked:
            n_missing = jnp.maximum(WIN_CHUNKS - (i * (tq // cq) + ci), 0) * CHUNK
            valid = coli >= n_missing
        probs = []
        for j, half, s in units:
            if masked:
                s = jnp.where(valid, s, -jnp.inf)
            sink = jnp.where(top, sink_ref[j * group + half], sink_ref[j * group + 2 + half])
            m = jnp.maximum(jnp.max(s, axis=-1, keepdims=True), sink)
            p = jnp.exp(s - m)
            den = jnp.sum(p, axis=-1, keepdims=True) + jnp.exp(sink - m)
            probs.append((p / den).astype(BF16))
        for j in range(SWA_KV_HEADS):
            c0 = j * group * SWA_HD
            vv = vbuf[r0:r0 + nk, j * LANES:(j + 1) * LANES].astype(BF16)
            acc = (jnp.dot(probs[2 * j], jnp.where(low, vv, zero), preferred_element_type=F32)
                   + jnp.dot(probs[2 * j + 1], jnp.where(low, zero, vv), preferred_element_type=F32))
            o_ref[r0:r0 + cq, c0:c0 + LANES] = acc[:cq].astype(o_ref.dtype)
            o_ref[r0:r0 + cq, c0 + LANES:c0 + 2 * LANES] = acc[cq:].astype(o_ref.dtype)

    pending = scores(0)
    for ci in range(1, tq // cq):
        nxt = scores(ci)
        finish(*pending)
        pending = nxt
    finish(*pending)


def _swa_call(sinks, q, k_prev, k_cur, v_prev, v_cur, prev_spec, batch, t, tq, cq, prev, mask_history):
    nt = t // tq
    tok = lambda b, i: (b * nt + i, 0)
    return pl.pallas_call(
        functools.partial(_swa_kernel, tq=tq, cq=cq, prev=prev, mask_history=mask_history),
        grid=(batch, nt),
        in_specs=[
            pl.BlockSpec(memory_space=pltpu.SMEM),
            pl.BlockSpec((tq, SWA_Q), tok),
            prev_spec,
            pl.BlockSpec((tq, SWA_KV_DUP), tok),
            prev_spec,
            pl.BlockSpec((tq, SWA_KV_DUP), tok),
        ],
        out_specs=pl.BlockSpec((tq, SWA_Q), tok),
        out_shape=jax.ShapeDtypeStruct((batch * t, SWA_Q), q.dtype),
        scratch_shapes=[pltpu.VMEM((prev + tq, SWA_KV_DUP), k_cur.dtype),
                        pltpu.VMEM((prev + tq, SWA_KV_DUP), v_cur.dtype)],
        compiler_params=_params(2),
        name="sliding_window_attention",
    )(sinks.astype(F32), q, k_prev, k_cur, v_prev, v_cur)


def _swa_prompt(sinks, q, k, v, batch, t):
    tq = min(TOKEN_TILE, t)
    nt = t // tq
    per = tq // WINDOW
    prev_spec = pl.BlockSpec((WINDOW, SWA_KV_DUP), lambda b, i: (jnp.maximum((b * nt + i) * per - 1, 0), 0))
    return _swa_call(sinks, q, k, k, v, v, prev_spec, batch, t, tq, CHUNK, WINDOW, True)


def _swa_sample(sinks, q, k, v, cache_k, cache_v, batch, t):
    n_keep = cache_k.shape[1]
    q_pos = PAST_LEN + np.arange(t)
    k_pos = np.concatenate([PAST_LEN - n_keep + np.arange(n_keep), q_pos])
    qc = (q_pos // CHUNK)[:, None]
    kc = (k_pos // CHUNK)[None, :]
    assert np.all((kc <= qc) & (kc >= qc - WIN_CHUNKS)), "sample step expects every cached row in window"
    prev_spec = pl.BlockSpec((n_keep, SWA_KV_DUP), lambda b, i: (b, 0))
    return _swa_call(sinks, q, cache_k.reshape(batch * n_keep, SWA_KV_DUP), k,
                     cache_v.reshape(batch * n_keep, SWA_KV_DUP), v, prev_spec, batch, t, t, t, n_keep, False)


def _dup_heads(t):
    return jnp.concatenate([t[..., 0, :], t[..., 0, :], t[..., 1, :], t[..., 1, :]], axis=-1)


def _undup_heads(t):
    return jnp.stack([t[:, :SWA_HD], t[:, 2 * SWA_HD:3 * SWA_HD]], axis=1).astype(F32)


def _memattn_kernel(q_ref, mk_ref, mv_ref, o_ref, *, rows):
    tt = q_ref.shape[0]

    loaded = {}

    def head(ref, h):
        if (id(ref), h) not in loaded:
            rows_h = ref[:, h, :] if len(ref.shape) == 3 else ref[:, h * XA_HD:(h + 1) * XA_HD]
            loaded[id(ref), h] = rows_h.astype(BF16)
        return loaded[id(ref), h]

    def scores(h, r0):
        ls = slice(h * XA_HD, (h + 1) * XA_HD)
        s = lax.dot_general(q_ref[r0:r0 + rows, ls].astype(BF16), head(mk_ref, h),
                            (((1,), (1,)), ((), ())), preferred_element_type=F32)
        return h, r0, s

    def finish(h, r0, s):
        ls = slice(h * XA_HD, (h + 1) * XA_HD)
        s = s * (XA_HD ** -0.5)
        m = jnp.max(s, axis=-1, keepdims=True)
        p = jnp.exp(s - m)
        p = (p / jnp.sum(p, axis=-1, keepdims=True)).astype(BF16)
        o_ref[r0:r0 + rows, ls] = jnp.dot(p, head(mv_ref, h),
                                          preferred_element_type=F32).astype(o_ref.dtype)

    blocks = [(h, r0) for h in range(XA_HEADS) for r0 in range(0, tt, rows)]
    pending = scores(*blocks[0])
    for blk in blocks[1:]:
        nxt = scores(*blk)
        finish(*pending)
        pending = nxt
    finish(*pending)


def _mem_attention(q, mk, mv, batch, t, layer):
    tt = min(TOKEN_TILE, t)
    nt = t // tt
    tok = lambda b, i: (b * nt + i, 0)
    if mk.ndim == 4:
        mem = pl.BlockSpec((None, None, N_MEM, XA_Q), lambda b, i: (layer, b, 0, 0))
    else:
        mem = pl.BlockSpec((None, None, N_MEM, XA_HEADS, XA_HD), lambda b, i: (layer, b, 0, 0, 0))
    return pl.pallas_call(
        functools.partial(_memattn_kernel, rows=min(256, tt)),
        grid=(batch, nt),
        in_specs=[pl.BlockSpec((tt, XA_Q), tok), mem, mem],
        out_specs=pl.BlockSpec((tt, XA_Q), tok),
        out_shape=jax.ShapeDtypeStruct((batch * t, XA_Q), q.dtype),
        compiler_params=_params(2),
        name="memory_attention",
    )(q, mk, mv)


MERGE_ROWS = 256


def _merge_kernel(x_ref, og_ref, os_ref, ox_ref, gate_ref, wb_ref, wo_ref, g_ref, b_ref, out_ref):
    rows = min(MERGE_ROWS, x_ref.shape[0])

    def branches(r0):
        rs = slice(r0, r0 + rows)
        return r0, [jnp.dot(ref[rs, :].astype(BF16), wb_ref[bi], preferred_element_type=F32)
                    for bi, ref in enumerate((og_ref, os_ref, ox_ref))]

    def finish(r0, ys):
        rs = slice(r0, r0 + rows)
        merged = None
        for bi, y in enumerate(ys):
            y = gate_ref[rs, bi * D_MODEL:(bi + 1) * D_MODEL].astype(F32) * y
            merged = y if merged is None else merged + y
        u = jnp.dot(merged.astype(BF16), wo_ref[...], preferred_element_type=F32)
        out_ref[rs, :] = _layer_norm(DN_ALPHA * x_ref[rs, :] + u, g_ref[...], b_ref[...])

    pending = branches(0)
    for r0 in range(rows, x_ref.shape[0], rows):
        nxt = branches(r0)
        finish(*pending)
        pending = nxt
    finish(*pending)


def _merge(x2d, o_g, o_s, o_x, gate, w_branch, w_o, layer, ln_g, ln_b):
    m = x2d.shape[0]
    tm = min(TOKEN_TILE, m)
    row = lambda w: pl.BlockSpec((tm, w), lambda i: (i, 0))
    return pl.pallas_call(
        _merge_kernel,
        grid=(m // tm,),
        in_specs=[row(D_MODEL), row(BRANCH_W), row(BRANCH_W), row(BRANCH_W), row(N_BRANCH * D_MODEL),
                  _layer_spec((N_BRANCH, BRANCH_W, D_MODEL), layer), _layer_spec((D_MODEL, D_MODEL), layer),
                  _const_spec((1, D_MODEL)), _const_spec((1, D_MODEL))],
        out_specs=row(D_MODEL),
        out_shape=jax.ShapeDtypeStruct((m, D_MODEL), F32),
        compiler_params=_params(1),
        name="branch_merge",
    )(x2d, o_g, o_s, o_x, gate, w_branch, w_o, ln_g.reshape(1, D_MODEL), ln_b.reshape(1, D_MODEL))


FF_CHUNK = 1024


def _ffn_kernel(x_ref, wu_ref, wd_ref, g_ref, b_ref, out_ref):
    x = x_ref[...]
    xb = x.astype(BF16)
    def up(c0):
        return c0, jnp.dot(xb, wu_ref[:, c0:c0 + FF_CHUNK], preferred_element_type=F32)

    def down(c0, hid):
        hid = jnp.square(jnp.maximum(hid, 0.0)).astype(BF16)
        return jnp.dot(hid, wd_ref[c0:c0 + FF_CHUNK, :], preferred_element_type=F32)

    pending = up(0)
    acc = None
    for c0 in range(FF_CHUNK, D_FF, FF_CHUNK):
        nxt = up(c0)
        part = down(*pending)
        acc = part if acc is None else acc + part
        pending = nxt
    part = down(*pending)
    acc = part if acc is None else acc + part
    out_ref[...] = _layer_norm(DN_ALPHA * x + acc, g_ref[...], b_ref[...])


def _ffn(x2d, w_up, w_down, layer, ln_g, ln_b):
    m = x2d.shape[0]
    tm = min(TOKEN_TILE, m)
    row = pl.BlockSpec((tm, D_MODEL), lambda i: (i, 0))
    return pl.pallas_call(
        _ffn_kernel,
        grid=(m // tm,),
        in_specs=[row, _layer_spec((D_MODEL, D_FF), layer), _layer_spec((D_FF, D_MODEL), layer),
                  _const_spec((1, D_MODEL)), _const_spec((1, D_MODEL))],
        out_specs=row,
        out_shape=jax.ShapeDtypeStruct((m, D_MODEL), F32),
        compiler_params=_params(1),
        name="channel_mixer",
    )(x2d, w_up, w_down, ln_g.reshape(1, D_MODEL), ln_b.reshape(1, D_MODEL))


def _layer(x2d, batch, t, act_dtype, lw, conv0, s0, mk, mv, swa_cache):
    if conv0 is None:
        h = _in_projection(x2d, lw["w_in"], lw["layer"], act_dtype, lw["conv_w"], t)
        tiles = t // min(TOKEN_TILE, t)
        conv_new = h["conv_tail"].reshape(batch, tiles, CONV_HIST, GDN_CONV_CH)[:, -1, CONV_HIST - (CONV_W - 1):]
    else:
        h = _in_projection(x2d, lw["w_in"], lw["layer"], act_dtype)
    o_g, s_new, conv_out = _gdn(h["qkv"], h["z"], h["ab"], conv0, s0, lw["conv_w"], lw["a_log"], lw["dt_bias"],
                                lw["gdn_norm_w"], batch, t)
    if conv0 is not None:
        conv_new = conv_out
    if swa_cache is None:
        o_s = _swa_prompt(lw["attn_sinks"], h["qs"], h["ks"], h["vs"], batch, t)
    else:
        o_s = _swa_sample(lw["attn_sinks"], h["qs"], h["ks"], h["vs"], swa_cache[0], swa_cache[1], batch, t)
    o_x = _mem_attention(h["qx"], mk, mv, batch, t, lw["layer"])
    x1 = _merge(x2d, o_g, o_s, o_x, h["gate"], lw["w_branch"], lw["w_o"], lw["layer"], lw["ln1_g"], lw["ln1_b"])
    x2 = _ffn(x1, lw["w_up"], lw["w_down"], lw["layer"], lw["ln2_g"], lw["ln2_b"])
    return x2, s_new, conv_new, h["ks"], h["vs"]


def kernel(x_prompt, x_sample, state_gdn_s, state_gdn_conv, cache_swa_k, cache_swa_v, cache_mem_k, cache_mem_v,
           mem_prompt, w_in, conv_w, a_log, dt_bias, gdn_norm_w, attn_sinks, w_mem_kv, w_branch, w_o,
           ln1_g, ln1_b, w_up, w_down, ln2_g, ln2_b):
    depth = w_in.shape[0]
    stacked = dict(w_in=_prep_w_in(w_in), w_mem_kv=w_mem_kv.astype(BF16), w_branch=w_branch.astype(BF16),
                   w_o=w_o.astype(BF16), w_up=w_up.astype(BF16), w_down=w_down.astype(BF16))
    layers = []
    for l in range(depth):
        layers.append(dict(
            stacked, layer=l, conv_w=conv_w[l], a_log=a_log[l], dt_bias=dt_bias[l],
            gdn_norm_w=gdn_norm_w[l], attn_sinks=attn_sinks[l], ln1_g=ln1_g[l], ln1_b=ln1_b[l],
            ln2_g=ln2_g[l], ln2_b=ln2_b[l]))

    bp, tp, _ = x_prompt.shape
    n_keep = cache_swa_k.shape[2]
    x = x_prompt.reshape(bp * tp, D_MODEL)
    mem2d = mem_prompt.reshape(bp * N_MEM, D_MODEL)
    p_s, p_conv, p_k, p_v = [], [], [], []
    mk, mv, mk_b, mv_b = _mem_kv(mem2d, stacked["w_mem_kv"])
    mk, mv = [a.reshape(depth, bp, N_MEM, XA_HEADS, XA_HD) for a in (mk, mv)]
    mk_b, mv_b = [a.reshape(depth, bp, N_MEM, XA_Q) for a in (mk_b, mv_b)]
    for lw in layers:
        s0 = jnp.zeros((bp, GDN_HEADS, GDN_DK, GDN_DV), F32)
        x, s_new, conv_new, ks, vs = _layer(x, bp, tp, BF16, lw, None, s0, mk_b, mv_b, None)
        p_s.append(s_new)
        p_conv.append(conv_new)
        p_k.append(_undup_heads(ks.reshape(bp, tp, SWA_KV_DUP)[:, tp - n_keep:].reshape(bp * n_keep, SWA_KV_DUP))
                   .reshape(bp, n_keep, SWA_KV_HEADS, SWA_HD))
        p_v.append(_undup_heads(vs.reshape(bp, tp, SWA_KV_DUP)[:, tp - n_keep:].reshape(bp * n_keep, SWA_KV_DUP))
                   .reshape(bp, n_keep, SWA_KV_HEADS, SWA_HD))
    y_prompt = x.reshape(bp, tp, D_MODEL)

    bs, ts, _ = x_sample.shape
    x = x_sample.reshape(bs * ts, D_MODEL)
    s_s, s_conv, s_k, s_v = [], [], [], []
    for l, lw in enumerate(layers):
        cache = (_dup_heads(cache_swa_k[l]), _dup_heads(cache_swa_v[l]))
        x, s_new, conv_new, ks, vs = _layer(
            x, bs, ts, F32, lw, state_gdn_conv[l], state_gdn_s[l],
            cache_mem_k, cache_mem_v, cache)
        s_s.append(s_new)
        s_conv.append(conv_new)
        k_new = _undup_heads(ks).reshape(bs, ts, SWA_KV_HEADS, SWA_HD)
        v_new = _undup_heads(vs).reshape(bs, ts, SWA_KV_HEADS, SWA_HD)
        s_k.append(jnp.concatenate([cache_swa_k[l], k_new], axis=1)[:, -n_keep:])
        s_v.append(jnp.concatenate([cache_swa_v[l], v_new], axis=1)[:, -n_keep:])
    y_sample = x.reshape(bs, ts, D_MODEL)

    return (y_prompt, y_sample,
            jnp.stack(p_s), jnp.stack(p_conv), jnp.stack(p_k), jnp.stack(p_v), mk, mv,
            jnp.stack(s_s), jnp.stack(s_conv), jnp.stack(s_k), jnp.stack(s_v))
```

```python
import functools

import jax
import jax.numpy as jnp
import numpy as np
from jax import lax
from jax.experimental import pallas as pl
from jax.experimental.pallas import tpu as pltpu

F32 = jnp.float32
BF16 = jnp.bfloat16
HI = lax.Precision.HIGHEST

D_MODEL = 1024
DEPTH = 2
CHUNK = 64
PAST_LEN = 2048
GDN_HEADS = 4
GDN_DK = 128
GDN_DV = 128
CONV_W = 4
SWA_HEADS = 8
SWA_KV_HEADS = 2
SWA_HD = 64
WINDOW = 128
WIN_CHUNKS = WINDOW // CHUNK
SWA_SCALE = SWA_HD ** -0.5
assert np.log2(SWA_SCALE) == round(np.log2(SWA_SCALE)), "the scale is folded into q before a bf16 cast"
N_MEM = 256
XA_HEADS = 4
XA_HD = 128
D_FF = 4 * D_MODEL
N_BRANCH = 3
BRANCH_W = 512
DN_ALPHA = (2.0 * DEPTH) ** 0.25
LN_EPS = 1e-5
NORM_EPS = 1e-6
GDN_QK = GDN_HEADS * GDN_DK
GDN_V = GDN_HEADS * GDN_DV
GDN_CONV_CH = 2 * GDN_QK + GDN_V
SWA_Q = SWA_HEADS * SWA_HD
SWA_KV = SWA_KV_HEADS * SWA_HD
XA_Q = XA_HEADS * XA_HD
IN_SPLITS = (GDN_CONV_CH, GDN_HEADS, GDN_HEADS, GDN_V, SWA_Q, SWA_KV, SWA_KV, XA_Q, N_BRANCH * D_MODEL)

LANES = 128
SUBLANES = 8
TOKEN_TILE = 512
VMEM_LIMIT = 56 * 2**20

SWA_KV_DUP = 2 * SWA_KV
AB_PAD = LANES
PROJ_OUT = (("qkv", GDN_CONV_CH), ("z", GDN_V), ("qs", SWA_Q), ("qx", XA_Q), ("ks", SWA_KV_DUP),
            ("vs", SWA_KV_DUP), ("ab", AB_PAD), ("gate", N_BRANCH * D_MODEL))
PROJ_COLS = sum(w for _, w in PROJ_OUT)
PROJ_CHUNK = 512
CONV_CHUNK = 256


def _params(n_grid):
    return pltpu.CompilerParams(dimension_semantics=("arbitrary",) * n_grid, vmem_limit_bytes=VMEM_LIMIT)


def _const_spec(shape):
    zeros = (0,) * len(shape)
    return pl.BlockSpec(shape, lambda *_: zeros, pipeline_mode=pl.Buffered(1))


def _layer_spec(shape, layer):
    zeros = (0,) * len(shape)
    return pl.BlockSpec((None,) + tuple(shape), lambda *_: (layer,) + zeros, pipeline_mode=pl.Buffered(1))


def _sigmoid(x):
    return 1.0 / (1.0 + jnp.exp(-x))


def _softplus(x):
    return jnp.maximum(x, 0.0) + jnp.log1p(jnp.exp(-jnp.abs(x)))


def _layer_norm(r, g, b):
    mu = jnp.mean(r, axis=-1, keepdims=True)
    d = r - mu
    var = jnp.mean(d * d, axis=-1, keepdims=True)
    return d * lax.rsqrt(var + LN_EPS) * g + b


CONV_HIST = SUBLANES
CONV_ROWS = 128


def _zero_after(y, r0=0):
    bits = pltpu.bitcast(y[r0:r0 + SUBLANES, 0:LANES], jnp.uint32)
    return ((bits >> 16) >> 16)[0:1, :].astype(F32)


def _conv_silu_norm(cw_ref, buf, r0, rows, cg, buf_lane0=0, anchor=None):
    ls = slice(cg * LANES, (cg + 1) * LANES)
    bl = slice(cg * LANES - buf_lane0, (cg + 1) * LANES - buf_lane0)
    acc = None
    for w in range(CONV_W):
        s = CONV_HIST - (CONV_W - 1) + w + r0
        tap = cw_ref[w:w + 1, ls]
        if w == 0 and anchor is not None:
            tap = tap + anchor
        term = tap * buf[s:s + rows, bl]
        acc = term if acc is None else acc + term
    y = acc * _sigmoid(acc)
    if cg < 2 * GDN_HEADS:
        inv = lax.rsqrt(jnp.sum(y * y, axis=-1, keepdims=True) + NORM_EPS)
        if cg < GDN_HEADS:
            inv = inv * (GDN_DK ** -0.5)
        y = y * inv
    return y


def _inproj_kernel(*refs, tiles_per_stream):
    fuse_conv = tiles_per_stream is not None
    n_out = len(PROJ_OUT)
    x_ref, w_refs, refs = refs[0], refs[1:1 + n_out], refs[1 + n_out:]
    if fuse_conv:
        cw_ref = refs[0]
        out_refs, tail_ref, ybufs, xb_s = refs[1:1 + n_out], refs[1 + n_out], refs[2 + n_out:-1], refs[-1]

        @pl.when(pl.program_id(0) % tiles_per_stream == 0)
        def _():
            for ybuf in ybufs:
                ybuf[0:CONV_HIST, :] = jnp.zeros((CONV_HIST, CONV_CHUNK), F32)
    else:
        out_refs, xb_s = refs[:-1], refs[-1]
    tm = x_ref.shape[0]
    xb_s[...] = x_ref[...].astype(BF16)
    chunks = []
    for (name, width), ref, w_ref in zip(PROJ_OUT, out_refs, w_refs):
        step = CONV_CHUNK if (fuse_conv and name == "qkv") else PROJ_CHUNK
        chunks += [(name, ref, w_ref, c0, min(step, width - c0)) for c0 in range(0, width, step)]
    if fuse_conv:
        pre = [ch for ch in chunks if ch[0] == "qkv"]
        rest = [ch for ch in chunks if ch[0] != "qkv"]
        per = -(-len(rest) // len(pre))
        chunks = []
        for n, ch in enumerate(pre):
            chunks += [ch] + rest[n * per:(n + 1) * per]
    pending = []
    for n, (name, ref, w_ref, c0, cw) in enumerate(chunks):
        y = jnp.dot(xb_s[...], w_ref[:, c0:c0 + cw], preferred_element_type=F32)
        if name == "qkv" and fuse_conv:
            ybuf = ybufs[c0 // CONV_CHUNK]
            ybuf[CONV_HIST:CONV_HIST + tm, :] = y
            tail_ref[:, c0:c0 + cw] = ybuf[tm:tm + CONV_HIST, :]
            pending = [(ybuf, c0, cg, r0) for cg in range(c0 // LANES, (c0 + cw) // LANES)
                       for r0 in range(0, tm, CONV_ROWS)]
            n_follow = len([ch for ch in chunks[n + 1:n + 1 + per] if ch[0] != "qkv"])
            share = -(-len(pending) // max(n_follow, 1))
            continue
        if name == "gate":
            y = _sigmoid(y)
        ref[:, c0:c0 + cw] = y.astype(ref.dtype)
        for k, (ybuf, lane0, cg, r0) in enumerate(pending[:share] if fuse_conv else []):
            anchor = _zero_after(y, (k * tm // share) // SUBLANES * SUBLANES)
            out_refs[0][r0:r0 + CONV_ROWS, cg * LANES:(cg + 1) * LANES] = _conv_silu_norm(
                cw_ref, ybuf, r0, CONV_ROWS, cg, lane0, anchor).astype(out_refs[0].dtype)
        pending = pending[share:] if fuse_conv else pending
    assert not pending
    if fuse_conv:
        for ybuf in ybufs:
            ybuf[0:CONV_HIST, :] = ybuf[tm:tm + CONV_HIST, :]


def _in_projection(x2d, w_all, layer, act_dtype, conv_w=None, stream_len=None):
    m = x2d.shape[0]
    tm = min(TOKEN_TILE, m)
    fuse_conv = conv_w is not None
    dtypes = {"ab": F32}
    out_shape = [jax.ShapeDtypeStruct((m, w), dtypes.get(n, act_dtype)) for n, w in PROJ_OUT]
    out_specs = [pl.BlockSpec((tm, w), lambda i: (i, 0)) for _, w in PROJ_OUT]
    in_specs = [pl.BlockSpec((tm, D_MODEL), lambda i: (i, 0))] + [_layer_spec((D_MODEL, w), layer) for _, w in PROJ_OUT]
    args = [x2d] + [w_all[n] for n, _ in PROJ_OUT]
    names = [n for n, _ in PROJ_OUT]
    scratch = []
    if fuse_conv:
        assert stream_len % tm == 0 and tm % CONV_ROWS == 0
        in_specs.append(_const_spec((CONV_W, GDN_CONV_CH)))
        args.append(conv_w.astype(F32))
        out_shape.append(jax.ShapeDtypeStruct((m // tm * CONV_HIST, GDN_CONV_CH), F32))
        out_specs.append(pl.BlockSpec((CONV_HIST, GDN_CONV_CH), lambda i: (i, 0)))
        names.append("conv_tail")
        assert GDN_CONV_CH % CONV_CHUNK == 0
        scratch += [pltpu.VMEM((CONV_HIST + tm, CONV_CHUNK), F32) for _ in range(GDN_CONV_CH // CONV_CHUNK)]
    scratch.append(pltpu.VMEM((tm, D_MODEL), BF16))
    outs = pl.pallas_call(
        functools.partial(_inproj_kernel, tiles_per_stream=stream_len // tm if fuse_conv else None),
        grid=(m // tm,),
        in_specs=in_specs,
        out_specs=out_specs,
        out_shape=out_shape,
        scratch_shapes=scratch,
        compiler_params=_params(1),
        name="in_projection",
    )(*args)
    return dict(zip(names, outs))


def _prep_w_in(w):
    offs = np.cumsum((0,) + IN_SPLITS)
    qkv, a, b, z, qs, ks, vs, qx, gate = [w[..., offs[i]:offs[i + 1]] for i in range(len(IN_SPLITS))]

    def dup(t):
        return [t[..., :SWA_HD], t[..., :SWA_HD], t[..., SWA_HD:], t[..., SWA_HD:]]

    pad = jnp.zeros(w.shape[:-1] + (AB_PAD - 2 * GDN_HEADS,), w.dtype)
    groups = dict(qkv=qkv, z=z, qs=qs, qx=qx, ks=jnp.concatenate(dup(ks), axis=-1),
                  vs=jnp.concatenate(dup(vs), axis=-1), ab=jnp.concatenate([a, b, pad], axis=-1), gate=gate)
    assert all(groups[n].shape[-1] == width for n, width in PROJ_OUT)
    return {n: g.astype(BF16) for n, g in groups.items()}


def _mem_kv_kernel(x_ref, w_ref, mk_ref, mv_ref, mkb_ref, mvb_ref):
    y = jnp.dot(x_ref[...].astype(BF16), w_ref[...], preferred_element_type=F32)
    mkb_ref[...] = y[:, :XA_Q].astype(BF16)
    mvb_ref[...] = y[:, XA_Q:].astype(BF16)
    for h in range(XA_HEADS):
        mk_ref[:, h, :] = y[:, h * XA_HD:(h + 1) * XA_HD]
        mv_ref[:, h, :] = y[:, XA_Q + h * XA_HD:XA_Q + (h + 1) * XA_HD]


def _mem_kv(mem2d, w_stacked):
    m, k = mem2d.shape
    depth = w_stacked.shape[0]
    tm = min(TOKEN_TILE, m)
    out = pl.BlockSpec((None, tm, XA_HEADS, XA_HD), lambda l, i: (l, i, 0, 0))
    shape = jax.ShapeDtypeStruct((depth, m, XA_HEADS, XA_HD), F32)
    out_b = pl.BlockSpec((None, tm, XA_Q), lambda l, i: (l, i, 0))
    shape_b = jax.ShapeDtypeStruct((depth, m, XA_Q), BF16)
    return pl.pallas_call(
        _mem_kv_kernel,
        grid=(depth, m // tm),
        in_specs=[pl.BlockSpec((tm, k), lambda l, i: (i, 0)),
                  pl.BlockSpec((None, k, 2 * XA_Q), lambda l, i: (l, 0, 0))],
        out_specs=[out, out, out_b, out_b],
        out_shape=[shape, shape, shape_b, shape_b],
        compiler_params=_params(2),
        name="mem_kv_projection",
    )(mem2d, w_stacked)


def _gdn_kernel(*refs, tt, c, ua, conv_done):
    refs = list(refs)
    if conv_done:
        (qkv_ref, z_ref, ab_ref, s0_ref, alog_ref, dtb_ref, nw_ref, o_ref, sout_ref,
         gbuf, bbuf, s_scr, u_s, w_s, qe_s, kdt_s, qk_s, el_s) = refs
    else:
        (qkv_ref, z_ref, ab_ref, conv0_ref, s0_ref, cw_ref, alog_ref, dtb_ref, nw_ref, o_ref, sout_ref, convout_ref,
         xbuf, cbuf, gbuf, bbuf, s_scr, u_s, w_s, qe_s, kdt_s, qk_s, el_s) = refs
    i = pl.program_id(1)

    @pl.when(i == 0)
    def _():
        s_scr[...] = s0_ref[...]

    if not conv_done:
        @pl.when(i == 0)
        def _():
            xbuf[0:CONV_HIST, :] = conv0_ref[...]

        @pl.when(i > 0)
        def _():
            xbuf[0:CONV_HIST, :] = xbuf[tt:tt + CONV_HIST, :]

        xbuf[CONV_HIST:CONV_HIST + tt, :] = qkv_ref[...].astype(F32)

    def conv_out(rows, cols):
        return qkv_ref[rows, cols].astype(F32) if conv_done else cbuf[rows, cols]

    expalog = jnp.exp(alog_ref[...])
    dtb = dtb_ref[...]

    def prepass(ci):
        r0 = ci * c
        if not conv_done:
            for cg in range(GDN_CONV_CH // LANES):
                cbuf[r0:r0 + c, cg * LANES:(cg + 1) * LANES] = _conv_silu_norm(cw_ref, xbuf, r0, c, cg)
        ab = ab_ref[r0:r0 + c, :]
        gbuf[r0:r0 + c, :] = -expalog * _softplus(ab + dtb)
        bbuf[r0:r0 + c, :] = _sigmoid(ab)

    pairs = GDN_HEADS // 2
    row = lax.broadcasted_iota(jnp.int32, (c, 2 * c), 0)
    lane = lax.broadcasted_iota(jnp.int32, (c, 2 * c), 1)
    left = lane < c
    col = jnp.where(left, lane, lane - c)
    causal = row >= col
    strict = row > col
    l_strict = strict.astype(F32)
    l_incl2 = causal.astype(BF16)
    n_square = max(int(np.ceil(np.log2(c))) - 1, 0)
    nw = nw_ref[...]
    nt_dims = (((1,), (1,)), ((), ()))

    def split2(x):
        hi = x.astype(BF16)
        lo = (x - hi.astype(F32)).astype(BF16)
        return hi, lo

    def block_diag(a, b):
        zero = jnp.zeros_like(a)
        return jnp.concatenate([jnp.concatenate([a, zero], axis=1), jnp.concatenate([zero, b], axis=1)], axis=0)

    def pair_diag(p):
        zero = jnp.zeros_like(p)
        return jnp.concatenate([jnp.where(left, p, zero), jnp.where(left, zero, p)], axis=0)

    def phase_a(chunk_ids):
        chunks = []
        for ci in chunk_ids:
            rows = slice(ci * c, (ci + 1) * c)
            g_hi, g_lo = split2(gbuf[rows, :])
            gc_all = jnp.dot(l_incl2, jnp.concatenate([g_hi, g_lo], axis=0), preferred_element_type=F32)
            chunks.append((ci, rows, g_hi.astype(F32), g_lo.astype(F32), gc_all))
        yield
        chains = []
        for ci, rows, g_hi, g_lo, gc_all in chunks:
            el_s[ci * SUBLANES:(ci + 1) * SUBLANES, :] = jnp.broadcast_to(jnp.exp(gc_all[c - 1:c, :]),
                                                                         (SUBLANES, LANES))
            b_all = bbuf[rows, :]
            for hp in range(pairs):
                ha, hb = 2 * hp, 2 * hp + 1
                beta = [b_all[:, GDN_HEADS + h:GDN_HEADS + h + 1] for h in (ha, hb)]
                steps = jnp.concatenate(
                    [(jnp.where(left, g[:, ha:ha + 1], g[:, hb:hb + 1]) * l_strict).astype(BF16)
                     for g in (g_hi, g_lo)], axis=0)
                diff = jnp.dot(l_incl2, steps, preferred_element_type=F32)
                q2 = conv_out(rows, slice(ha * GDN_DK, (hb + 1) * GDN_DK))
                k = [conv_out(rows, slice(GDN_QK + h * GDN_DK, GDN_QK + (h + 1) * GDN_DK)) for h in (ha, hb)]
                kb2 = jnp.concatenate([k[0] * beta[0], k[1] * beta[1]], axis=1)
                gram = lax.dot_general(jnp.concatenate([kb2.astype(BF16), q2.astype(BF16)], axis=0),
                                       block_diag(k[0].astype(BF16), k[1].astype(BF16)), nt_dims,
                                       preferred_element_type=F32)
                chains.append(dict(ci=ci, rows=rows, hp=hp, heads=(ha, hb), beta=beta, diff=diff, gram=gram,
                                   gc=[gc_all[:, h:h + 1] for h in (ha, hb)],
                                   g_last=[gc_all[c - 1:c, h:h + 1] for h in (ha, hb)]))
        yield
        for ch in chains:
            decay = jnp.where(causal, jnp.exp(ch.pop("diff")), 0.0)
            gram = ch.pop("gram")
            qk_s[ch["ci"] * pairs + ch["hp"]] = (gram[c:] * decay).astype(BF16)
            ch["toff"] = -jnp.where(strict, gram[:c] * decay, 0.0)
            ch["pw"] = ch["toff"].astype(BF16)
        for _ in range(n_square):
            for ch in chains:
                ch["pw"] = jnp.dot(ch["pw"], pair_diag(ch["pw"]), preferred_element_type=F32)
            yield
            for ch in chains:
                pwb = ch["pw"].astype(BF16)
                ch["toff"] = ch["toff"] + ch["pw"] + jnp.dot(ch["toff"].astype(BF16), pair_diag(pwb),
                                                             preferred_element_type=F32)
                ch["pw"] = pwb
            yield
        for ch in chains:
            rows = ch["rows"]
            rhs, kdec = [], []
            for n, h in enumerate(ch["heads"]):
                egc = jnp.exp(ch["gc"][n])
                k = conv_out(rows, slice(GDN_QK + h * GDN_DK, GDN_QK + (h + 1) * GDN_DK))
                v = conv_out(rows, slice(2 * GDN_QK + h * GDN_DV, 2 * GDN_QK + (h + 1) * GDN_DV))
                rhs.append(jnp.concatenate([v * ch["beta"][n], k * (ch["beta"][n] * egc)], axis=1))
                kdec.append(k * jnp.exp(ch["g_last"][n] - ch["gc"][n]))
                qe_s[rows, h * GDN_DK:(h + 1) * GDN_DK] = (conv_out(rows, slice(h * GDN_DK, (h + 1) * GDN_DK))
                                                           * egc).astype(BF16)
            uw = jnp.dot(ch["toff"].astype(BF16), block_diag(rhs[0].astype(BF16), rhs[1].astype(BF16)),
                         preferred_element_type=F32)
            for n, h in enumerate(ch["heads"]):
                hq = slice(h * GDN_DK, (h + 1) * GDN_DK)
                uw_h = rhs[n] + uw[:, n * (GDN_DV + GDN_DK):(n + 1) * (GDN_DV + GDN_DK)]
                u_s[rows, hq] = uw_h[:, :GDN_DV]
                w_s[rows, hq] = uw_h[:, GDN_DV:].astype(BF16)
            kdt_s[ch["ci"] * pairs + ch["hp"]] = jnp.concatenate(kdec, axis=0).T.astype(BF16)
        yield

    heads = range(GDN_HEADS)
    hqs = [slice(h * GDN_DK, (h + 1) * GDN_DK) for h in heads]
    state = [s_scr[h] for h in heads]

    def phase_b(chunk_ids):
        for ci in chunk_ids:
            rows = slice(ci * c, (ci + 1) * c)
            e_last = el_s[ci * SUBLANES:ci * SUBLANES + 1, :]
            ws = [jnp.dot(jnp.concatenate([w_s[rows, hqs[h]], qe_s[rows, hqs[h]]], axis=0),
                          state[h].astype(BF16), preferred_element_type=F32) for h in heads]
            yield
            v_new = [(u_s[rows, hqs[h]] - ws[h][:c]).astype(BF16) for h in heads]
            v_diag = [block_diag(v_new[2 * hp], v_new[2 * hp + 1]) for hp in range(pairs)]
            s_inc = [jnp.dot(kdt_s[ci * pairs + hp], v_diag[hp], preferred_element_type=F32)
                     for hp in range(pairs)]
            o_intra = [jnp.dot(qk_s[ci * pairs + hp], v_diag[hp], preferred_element_type=F32)
                       for hp in range(pairs)]
            yield
            for h in heads:
                hp, side = divmod(h, 2)
                lanes = slice(side * GDN_DV, (side + 1) * GDN_DV)
                state[h] = state[h] * e_last[:, h:h + 1] + s_inc[hp][:, lanes]
                o = ws[h][c:] + o_intra[hp][:, lanes]
                o = o * lax.rsqrt(jnp.mean(o * o, axis=-1, keepdims=True) + NORM_EPS) * nw
                zz = z_ref[rows, hqs[h]].astype(F32)
                o_ref[rows, hqs[h]] = (o * (zz * _sigmoid(zz))).astype(o_ref.dtype)

    def alternate(*gens):
        gens = list(gens)
        while gens:
            for g in list(gens):
                if next(g, StopIteration) is StopIteration:
                    gens.remove(g)

    groups = [list(range(g0, g0 + ua)) for g0 in range(0, tt // c, ua)]
    def prepass_chunks(chunk_ids):
        for ci in chunk_ids:
            prepass(ci)
            yield

    alternate(prepass_chunks(groups[0]))
    for gi, grp in enumerate(groups):
        gens = [phase_a(grp)]
        if gi > 0:
            gens.append(phase_b(groups[gi - 1]))
        if gi + 1 < len(groups):
            gens.append(prepass_chunks(groups[gi + 1]))
        alternate(*gens)
    alternate(phase_b(groups[-1]))
    for h in heads:
        s_scr[h] = state[h]

    @pl.when(i == pl.num_programs(1) - 1)
    def _():
        sout_ref[...] = s_scr[...]
        if not conv_done:
            convout_ref[...] = xbuf[CONV_HIST + tt - (CONV_W - 1):CONV_HIST + tt, :]


GDN_PHASE_A_UNROLL = 4
GDN_TOKEN_TILE = 1024


def _gdn(qkv, z, ab, conv0, s0, conv_w, a_log, dt_bias, norm_w, batch, t):
    c = min(CHUNK, t)
    tt = min(GDN_TOKEN_TILE, t)
    nt = t // tt
    n_chunks = tt // c
    ua = min(GDN_PHASE_A_UNROLL, n_chunks)
    conv_done = conv0 is None

    def pad_vec(vv):
        return jnp.zeros((1, LANES), F32).at[0, :vv.shape[0]].set(vv.astype(F32))

    tok = lambda b, i: (b * nt + i, 0)
    state_spec = pl.BlockSpec((None, GDN_HEADS, GDN_DK, GDN_DV), lambda b, i: (b, 0, 0, 0))
    in_specs = [pl.BlockSpec((tt, GDN_CONV_CH), tok), pl.BlockSpec((tt, GDN_V), tok), pl.BlockSpec((tt, AB_PAD), tok)]
    args = [qkv, z, ab]
    if not conv_done:
        in_specs.append(pl.BlockSpec((None, CONV_HIST, GDN_CONV_CH), lambda b, i: (b, 0, 0)))
        args.append(jnp.concatenate([jnp.zeros((batch, CONV_HIST - (CONV_W - 1), GDN_CONV_CH), F32),
                                     conv0.astype(F32)], axis=1))
    in_specs.append(state_spec)
    args.append(s0.astype(F32))
    if not conv_done:
        in_specs.append(_const_spec((CONV_W, GDN_CONV_CH)))
        args.append(conv_w.astype(F32))
    in_specs += [_const_spec((1, LANES)), _const_spec((1, LANES)), _const_spec((1, GDN_DV))]
    args += [pad_vec(a_log), pad_vec(dt_bias), norm_w.astype(F32).reshape(1, GDN_DV)]
    out_specs = [pl.BlockSpec((tt, GDN_V), tok), state_spec]
    out_shape = [jax.ShapeDtypeStruct((batch * t, GDN_V), z.dtype),
                 jax.ShapeDtypeStruct((batch, GDN_HEADS, GDN_DK, GDN_DV), F32)]
    scratch = []
    if not conv_done:
        out_specs.append(pl.BlockSpec((None, CONV_W - 1, GDN_CONV_CH), lambda b, i: (b, 0, 0)))
        out_shape.append(jax.ShapeDtypeStruct((batch, CONV_W - 1, GDN_CONV_CH), F32))
        scratch += [pltpu.VMEM((tt + CONV_HIST, GDN_CONV_CH), F32),
                    pltpu.VMEM((tt, GDN_CONV_CH), F32)]
    scratch += [
        pltpu.VMEM((tt, LANES), F32),
        pltpu.VMEM((tt, LANES), F32),
        pltpu.VMEM((GDN_HEADS, GDN_DK, GDN_DV), F32),
        pltpu.VMEM((tt, GDN_V), F32),
        pltpu.VMEM((tt, GDN_QK), BF16),
        pltpu.VMEM((tt, GDN_QK), BF16),
        pltpu.VMEM((n_chunks * GDN_HEADS // 2, GDN_DK, 2 * c), BF16),
        pltpu.VMEM((n_chunks * GDN_HEADS // 2, c, 2 * c), BF16),
        pltpu.VMEM((n_chunks * SUBLANES, LANES), F32),
    ]
    outs = pl.pallas_call(
        functools.partial(_gdn_kernel, tt=tt, c=c, ua=ua, conv_done=conv_done),
        grid=(batch, nt),
        in_specs=in_specs,
        out_specs=out_specs,
        out_shape=out_shape,
        scratch_shapes=scratch,
        compiler_params=_params(2),
        name="gated_deltanet",
    )(*args)
    return outs[0], outs[1], (None if conv_done else outs[2])


def _swa_kernel(sink_ref, q_ref, kp_ref, kc_ref, vp_ref, vc_ref, o_ref, kbuf, vbuf, *, tq, cq, prev,
                mask_history):
    i = pl.program_id(1)
    kbuf[0:prev, :] = kp_ref[...]
    kbuf[prev:prev + tq, :] = kc_ref[...]
    vbuf[0:prev, :] = vp_ref[...]
    vbuf[prev:prev + tq, :] = vc_ref[...]
    nk = prev + cq
    group = SWA_HEADS // SWA_KV_HEADS
    lane = lax.broadcasted_iota(jnp.int32, (1, LANES), 1)
    low = lane < SWA_HD
    rowi = lax.broadcasted_iota(jnp.int32, (2 * cq, 1), 0)
    top = rowi < cq
    coli = lax.broadcasted_iota(jnp.int32, (1, nk), 1)
    zero = jnp.zeros((nk, LANES), BF16)

    def scores(ci):
        r0 = ci * cq
        units = []
        for j in range(SWA_KV_HEADS):
            c0 = j * group * SWA_HD
            qst = (jnp.concatenate([q_ref[r0:r0 + cq, c0:c0 + LANES],
                                    q_ref[r0:r0 + cq, c0 + LANES:c0 + 2 * LANES]], axis=0).astype(F32)
                   * SWA_SCALE).astype(BF16)
            kk = kbuf[r0:r0 + nk, j * LANES:(j + 1) * LANES].astype(BF16)
            for half in range(2):
                sel = low if half == 0 else jnp.logical_not(low)
                units.append((j, half, lax.dot_general(qst, jnp.where(sel, kk, zero), (((1,), (1,)), ((), ())),
                                                       preferred_element_type=F32)))
        return ci, units

    def finish(ci, units):
        r0 = ci * cq
        masked = mask_history and ci < WIN_CHUNKS
        if masked:
            n_missing = jnp.maximum(WIN_CHUNKS - (i * (tq // cq) + ci), 0) * CHUNK
            valid = coli >= n_missing
        probs = []
        for j, half, s in units:
            if masked:
                s = jnp.where(valid, s, -jnp.inf)
            sink = jnp.where(top, sink_ref[j * group + half], sink_ref[j * group + 2 + half])
            m = jnp.maximum(jnp.max(s, axis=-1, keepdims=True), sink)
            p = jnp.exp(s - m)
            den = jnp.sum(p, axis=-1, keepdims=True) + jnp.exp(sink - m)
            probs.append((p / den).astype(BF16))
        for j in range(SWA_KV_HEADS):
            c0 = j * group * SWA_HD
            vv = vbuf[r0:r0 + nk, j * LANES:(j + 1) * LANES].astype(BF16)
            acc = (jnp.dot(probs[2 * j], jnp.where(low, vv, zero), preferred_element_type=F32)
                   + jnp.dot(probs[2 * j + 1], jnp.where(low, zero, vv), preferred_element_type=F32))
            o_ref[r0:r0 + cq, c0:c0 + LANES] = acc[:cq].astype(o_ref.dtype)
            o_ref[r0:r0 + cq, c0 + LANES:c0 + 2 * LANES] = acc[cq:].astype(o_ref.dtype)

    pending = scores(0)
    for ci in range(1, tq // cq):
        nxt = scores(ci)
        finish(*pending)
        pending = nxt
    finish(*pending)


def _swa_call(sinks, q, k_prev, k_cur, v_prev, v_cur, prev_spec, batch, t, tq, cq, prev, mask_history):
    nt = t // tq
    tok = lambda b, i: (b * nt + i, 0)
    return pl.pallas_call(
        functools.partial(_swa_kernel, tq=tq, cq=cq, prev=prev, mask_history=mask_history),
        grid=(batch, nt),
        in_specs=[
            pl.BlockSpec(memory_space=pltpu.SMEM),
            pl.BlockSpec((tq, SWA_Q), tok),
            prev_spec,
            pl.BlockSpec((tq, SWA_KV_DUP), tok),
            prev_spec,
            pl.BlockSpec((tq, SWA_KV_DUP), tok),
        ],
        out_specs=pl.BlockSpec((tq, SWA_Q), tok),
        out_shape=jax.ShapeDtypeStruct((batch * t, SWA_Q), q.dtype),
        scratch_shapes=[pltpu.VMEM((prev + tq, SWA_KV_DUP), k_cur.dtype),
                        pltpu.VMEM((prev + tq, SWA_KV_DUP), v_cur.dtype)],
        compiler_params=_params(2),
        name="sliding_window_attention",
    )(sinks.astype(F32), q, k_prev, k_cur, v_prev, v_cur)


def _swa_prompt(sinks, q, k, v, batch, t):
    tq = min(TOKEN_TILE, t)
    nt = t // tq
    per = tq // WINDOW
    prev_spec = pl.BlockSpec((WINDOW, SWA_KV_DUP), lambda b, i: (jnp.maximum((b * nt + i) * per - 1, 0), 0))
    return _swa_call(sinks, q, k, k, v, v, prev_spec, batch, t, tq, CHUNK, WINDOW, True)


def _swa_sample(sinks, q, k, v, cache_k, cache_v, batch, t):
    n_keep = cache_k.shape[1]
    q_pos = PAST_LEN + np.arange(t)
    k_pos = np.concatenate([PAST_LEN - n_keep + np.arange(n_keep), q_pos])
    qc = (q_pos // CHUNK)[:, None]
    kc = (k_pos // CHUNK)[None, :]
    assert np.all((kc <= qc) & (kc >= qc - WIN_CHUNKS)), "sample step expects every cached row in window"
    prev_spec = pl.BlockSpec((n_keep, SWA_KV_DUP), lambda b, i: (b, 0))
    return _swa_call(sinks, q, cache_k.reshape(batch * n_keep, SWA_KV_DUP), k,
                     cache_v.reshape(batch * n_keep, SWA_KV_DUP), v, prev_spec, batch, t, t, t, n_keep, False)


def _dup_heads(t):
    return jnp.concatenate([t[..., 0, :], t[..., 0, :], t[..., 1, :], t[..., 1, :]], axis=-1)


def _undup_heads(t):
    return jnp.stack([t[:, :SWA_HD], t[:, 2 * SWA_HD:3 * SWA_HD]], axis=1).astype(F32)


def _memattn_kernel(q_ref, mk_ref, mv_ref, o_ref, *, rows):
    tt = q_ref.shape[0]

    loaded = {}

    def head(ref, h):
        if (id(ref), h) not in loaded:
            rows_h = ref[:, h, :] if len(ref.shape) == 3 else ref[:, h * XA_HD:(h + 1) * XA_HD]
            loaded[id(ref), h] = rows_h.astype(BF16)
        return loaded[id(ref), h]

    def scores(h, r0):
        ls = slice(h * XA_HD, (h + 1) * XA_HD)
        s = lax.dot_general(q_ref[r0:r0 + rows, ls].astype(BF16), head(mk_ref, h),
                            (((1,), (1,)), ((), ())), preferred_element_type=F32)
        return h, r0, s

    def finish(h, r0, s):
        ls = slice(h * XA_HD, (h + 1) * XA_HD)
        s = s * (XA_HD ** -0.5)
        m = jnp.max(s, axis=-1, keepdims=True)
        p = jnp.exp(s - m)
        p = (p / jnp.sum(p, axis=-1, keepdims=True)).astype(BF16)
        o_ref[r0:r0 + rows, ls] = jnp.dot(p, head(mv_ref, h),
                                          preferred_element_type=F32).astype(o_ref.dtype)

    blocks = [(h, r0) for h in range(XA_HEADS) for r0 in range(0, tt, rows)]
    pending = scores(*blocks[0])
    for blk in blocks[1:]:
        nxt = scores(*blk)
        finish(*pending)
        pending = nxt
    finish(*pending)


def _mem_attention(q, mk, mv, batch, t, layer):
    tt = min(TOKEN_TILE, t)
    nt = t // tt
    tok = lambda b, i: (b * nt + i, 0)
    if mk.ndim == 4:
        mem = pl.BlockSpec((None, None, N_MEM, XA_Q), lambda b, i: (layer, b, 0, 0))
    else:
        mem = pl.BlockSpec((None, None, N_MEM, XA_HEADS, XA_HD), lambda b, i: (layer, b, 0, 0, 0))
    return pl.pallas_call(
        functools.partial(_memattn_kernel, rows=min(256, tt)),
        grid=(batch, nt),
        in_specs=[pl.BlockSpec((tt, XA_Q), tok), mem, mem],
        out_specs=pl.BlockSpec((tt, XA_Q), tok),
        out_shape=jax.ShapeDtypeStruct((batch * t, XA_Q), q.dtype),
        compiler_params=_params(2),
        name="memory_attention",
    )(q, mk, mv)


MERGE_ROWS = 256


def _merge_kernel(x_ref, og_ref, os_ref, ox_ref, gate_ref, wb_ref, wo_ref, g_ref, b_ref, out_ref):
    rows = min(MERGE_ROWS, x_ref.shape[0])

    def branches(r0):
        rs = slice(r0, r0 + rows)
        return r0, [jnp.dot(ref[rs, :].astype(BF16), wb_ref[bi], preferred_element_type=F32)
                    for bi, ref in enumerate((og_ref, os_ref, ox_ref))]

    def finish(r0, ys):
        rs = slice(r0, r0 + rows)
        merged = None
        for bi, y in enumerate(ys):
            y = gate_ref[rs, bi * D_MODEL:(bi + 1) * D_MODEL].astype(F32) * y
            merged = y if merged is None else merged + y
        u = jnp.dot(merged.astype(BF16), wo_ref[...], preferred_element_type=F32)
        out_ref[rs, :] = _layer_norm(DN_ALPHA * x_ref[rs, :] + u, g_ref[...], b_ref[...])

    pending = branches(0)
    for r0 in range(rows, x_ref.shape[0], rows):
        nxt = branches(r0)
        finish(*pending)
        pending = nxt
    finish(*pending)


def _merge(x2d, o_g, o_s, o_x, gate, w_branch, w_o, layer, ln_g, ln_b):
    m = x2d.shape[0]
    tm = min(TOKEN_TILE, m)
    row = lambda w: pl.BlockSpec((tm, w), lambda i: (i, 0))
    return pl.pallas_call(
        _merge_kernel,
        grid=(m // tm,),
        in_specs=[row(D_MODEL), row(BRANCH_W), row(BRANCH_W), row(BRANCH_W), row(N_BRANCH * D_MODEL),
                  _layer_spec((N_BRANCH, BRANCH_W, D_MODEL), layer), _layer_spec((D_MODEL, D_MODEL), layer),
                  _const_spec((1, D_MODEL)), _const_spec((1, D_MODEL))],
        out_specs=row(D_MODEL),
        out_shape=jax.ShapeDtypeStruct((m, D_MODEL), F32),
        compiler_params=_params(1),
        name="branch_merge",
    )(x2d, o_g, o_s, o_x, gate, w_branch, w_o, ln_g.reshape(1, D_MODEL), ln_b.reshape(1, D_MODEL))


FF_CHUNK = 1024


FFN_ROWS = 256


def _ffn_kernel(x_ref, wu_ref, wd_ref, g_ref, b_ref, out_ref):
    rows = min(FFN_ROWS, x_ref.shape[0])

    def mixer(r0):
        rs = slice(r0, r0 + rows)
        x = x_ref[rs, :]
        xb = x.astype(BF16)

        def up(c0):
            return c0, jnp.dot(xb, wu_ref[:, c0:c0 + FF_CHUNK], preferred_element_type=F32)

        def down(c0, hid):
            hid = jnp.square(jnp.maximum(hid, 0.0)).astype(BF16)
            return jnp.dot(hid, wd_ref[c0:c0 + FF_CHUNK, :], preferred_element_type=F32)

        pending = up(0)
        yield
        acc = None
        for c0 in range(FF_CHUNK, D_FF, FF_CHUNK):
            nxt = up(c0)
            part = down(*pending)
            acc = part if acc is None else acc + part
            pending = nxt
        part = down(*pending)
        acc = part if acc is None else acc + part

        def finish():
            out_ref[rs, :] = _layer_norm(DN_ALPHA * x + acc, g_ref[...], b_ref[...])
        return finish

    finish_prev = None
    for r0 in range(0, x_ref.shape[0], rows):
        gen = mixer(r0)
        next(gen)
        if finish_prev is not None:
            finish_prev()
        try:
            next(gen)
        except StopIteration as done:
            finish_prev = done.value
    finish_prev()


def _ffn(x2d, w_up, w_down, layer, ln_g, ln_b):
    m = x2d.shape[0]
    tm = min(TOKEN_TILE, m)
    row = pl.BlockSpec((tm, D_MODEL), lambda i: (i, 0))
    return pl.pallas_call(
        _ffn_kernel,
        grid=(m // tm,),
        in_specs=[row, _layer_spec((D_MODEL, D_FF), layer), _layer_spec((D_FF, D_MODEL), layer),
                  _const_spec((1, D_MODEL)), _const_spec((1, D_MODEL))],
        out_specs=row,
        out_shape=jax.ShapeDtypeStruct((m, D_MODEL), F32),
        compiler_params=_params(1),
        name="channel_mixer",
    )(x2d, w_up, w_down, ln_g.reshape(1, D_MODEL), ln_b.reshape(1, D_MODEL))


def _layer(x2d, batch, t, act_dtype, lw, conv0, s0, mk, mv, swa_cache):
    if conv0 is None:
        h = _in_projection(x2d, lw["w_in"], lw["layer"], act_dtype, lw["conv_w"], t)
        tiles = t // min(TOKEN_TILE, t)
        conv_new = h["conv_tail"].reshape(batch, tiles, CONV_HIST, GDN_CONV_CH)[:, -1, CONV_HIST - (CONV_W - 1):]
    else:
        h = _in_projection(x2d, lw["w_in"], lw["layer"], act_dtype)
    o_g, s_new, conv_out = _gdn(h["qkv"], h["z"], h["ab"], conv0, s0, lw["conv_w"], lw["a_log"], lw["dt_bias"],
                                lw["gdn_norm_w"], batch, t)
    if conv0 is not None:
        conv_new = conv_out
    if swa_cache is None:
        o_s = _swa_prompt(lw["attn_sinks"], h["qs"], h["ks"], h["vs"], batch, t)
    else:
        o_s = _swa_sample(lw["attn_sinks"], h["qs"], h["ks"], h["vs"], swa_cache[0], swa_cache[1], batch, t)
    o_x = _mem_attention(h["qx"], mk, mv, batch, t, lw["layer"])
    x1 = _merge(x2d, o_g, o_s, o_x, h["gate"], lw["w_branch"], lw["w_o"], lw["layer"], lw["ln1_g"], lw["ln1_b"])
    x2 = _ffn(x1, lw["w_up"], lw["w_down"], lw["layer"], lw["ln2_g"], lw["ln2_b"])
    return x2, s_new, conv_new, h["ks"], h["vs"]


def kernel(x_prompt, x_sample, state_gdn_s, state_gdn_conv, cache_swa_k, cache_swa_v, cache_mem_k, cache_mem_v,
           mem_prompt, w_in, conv_w, a_log, dt_bias, gdn_norm_w, attn_sinks, w_mem_kv, w_branch, w_o,
           ln1_g, ln1_b, w_up, w_down, ln2_g, ln2_b):
    depth = w_in.shape[0]
    stacked = dict(w_in=_prep_w_in(w_in), w_mem_kv=w_mem_kv.astype(BF16), w_branch=w_branch.astype(BF16),
                   w_o=w_o.astype(BF16), w_up=w_up.astype(BF16), w_down=w_down.astype(BF16))
    layers = []
    for l in range(depth):
        layers.append(dict(
            stacked, layer=l, conv_w=conv_w[l], a_log=a_log[l], dt_bias=dt_bias[l],
            gdn_norm_w=gdn_norm_w[l], attn_sinks=attn_sinks[l], ln1_g=ln1_g[l], ln1_b=ln1_b[l],
            ln2_g=ln2_g[l], ln2_b=ln2_b[l]))

    bp, tp, _ = x_prompt.shape
    n_keep = cache_swa_k.shape[2]
    x = x_prompt.reshape(bp * tp, D_MODEL)
    mem2d = mem_prompt.reshape(bp * N_MEM, D_MODEL)
    p_s, p_conv, p_k, p_v = [], [], [], []
    mk, mv, mk_b, mv_b = _mem_kv(mem2d, stacked["w_mem_kv"])
    mk, mv = [a.reshape(depth, bp, N_MEM, XA_HEADS, XA_HD) for a in (mk, mv)]
    mk_b, mv_b = [a.reshape(depth, bp, N_MEM, XA_Q) for a in (mk_b, mv_b)]
    for lw in layers:
        s0 = jnp.zeros((bp, GDN_HEADS, GDN_DK, GDN_DV), F32)
        x, s_new, conv_new, ks, vs = _layer(x, bp, tp, BF16, lw, None, s0, mk_b, mv_b, None)
        p_s.append(s_new)
        p_conv.append(conv_new)
        p_k.append(_undup_heads(ks.reshape(bp, tp, SWA_KV_DUP)[:, tp - n_keep:].reshape(bp * n_keep, SWA_KV_DUP))
                   .reshape(bp, n_keep, SWA_KV_HEADS, SWA_HD))
        p_v.append(_undup_heads(vs.reshape(bp, tp, SWA_KV_DUP)[:, tp - n_keep:].reshape(bp * n_keep, SWA_KV_DUP))
                   .reshape(bp, n_keep, SWA_KV_HEADS, SWA_HD))
    y_prompt = x.reshape(bp, tp, D_MODEL)

    bs, ts, _ = x_sample.shape
    x = x_sample.reshape(bs * ts, D_MODEL)
    s_s, s_conv, s_k, s_v = [], [], [], []
    for l, lw in enumerate(layers):
        cache = (_dup_heads(cache_swa_k[l]), _dup_heads(cache_swa_v[l]))
        x, s_new, conv_new, ks, vs = _layer(
            x, bs, ts, F32, lw, state_gdn_conv[l], state_gdn_s[l],
            cache_mem_k, cache_mem_v, cache)
        s_s.append(s_new)
        s_conv.append(conv_new)
        k_new = _undup_heads(ks).reshape(bs, ts, SWA_KV_HEADS, SWA_HD)
        v_new = _undup_heads(vs).reshape(bs, ts, SWA_KV_HEADS, SWA_HD)
        s_k.append(jnp.concatenate([cache_swa_k[l], k_new], axis=1)[:, -n_keep:])
        s_v.append(jnp.concatenate([cache_swa_v[l], v_new], axis=1)[:, -n_keep:])
    y_sample = x.reshape(bs, ts, D_MODEL)

    return (y_prompt, y_sample,
            jnp.stack(p_s), jnp.stack(p_conv), jnp.stack(p_k), jnp.stack(p_v), mk, mv,
            jnp.stack(s_s), jnp.stack(s_conv), jnp.stack(s_k), jnp.stack(s_v))
```
